```python
import math
import jax, jax.numpy as jnp
from jax import lax
import numpy as np

D_MODEL = 4096
BATCH = 4
SEQ = 4096
DEPTH = 1

CHUNK = 64
D_FF = 256 * ((8 * D_MODEL // 3 + 255) // 256)
GMLP_WIDTH = D_MODEL // 2
GMLP_GROUP_DIM = 128
GMLP_GROUPS = GMLP_WIDTH // GMLP_GROUP_DIM
GMLP_BLOCK = 128
HEAD_DIM = 128
ATTN_HEADS = (D_MODEL // 2) // HEAD_DIM
ATTN_WIDTH = ATTN_HEADS * HEAD_DIM
IDX_HEADS = D_MODEL // 128
IDX_DIM = 64
TOPK_MAX = 256
Q_BLOCK = 128
ROPE_THETA = 10000.0
EPS = 1e-6
SPLITS = (GMLP_WIDTH, GMLP_WIDTH, ATTN_WIDTH, HEAD_DIM, HEAD_DIM,
          IDX_HEADS * IDX_DIM, IDX_DIM, IDX_HEADS, D_MODEL, D_MODEL)
IN_WIDTH = sum(SPLITS)

kernel_name = "hybrid_gmlp_dsa_macaron_block"


def rms_norm(x, g):
    xf = x.astype(jnp.float32)
    y = xf * lax.rsqrt(jnp.mean(xf * xf, axis=-1, keepdims=True) + EPS)
    return (y * g.astype(jnp.float32)).astype(x.dtype)


def rope_tables(seq, dim):
    inv = ROPE_THETA ** (-jnp.arange(0, dim, 2, dtype=jnp.float32) / dim)
    ang = jnp.arange(seq, dtype=jnp.float32)[:, None] * inv[None, :]
    return jnp.cos(ang), jnp.sin(ang)


def apply_rope(x, cos, sin):
    if x.ndim == 4:
        cos, sin = cos[:, None, :], sin[:, None, :]
    xf = x.astype(jnp.float32)
    x1, x2 = jnp.split(xf, 2, axis=-1)
    out = jnp.concatenate([x1 * cos - x2 * sin, x1 * sin + x2 * cos], axis=-1)
    return out.astype(x.dtype)


def swiglu(x, w1, w3, w2):
    return (jax.nn.silu(x @ w1) * (x @ w3)) @ w2


def gmlp_spatial_gating(u, v, v_gain, ws, bs):
    B, S, _ = u.shape
    u = jax.nn.gelu(u, approximate=False)
    v = rms_norm(jax.nn.gelu(v, approximate=False), v_gain)
    n_blk = S // GMLP_BLOCK
    pos_chunk = jnp.arange(GMLP_BLOCK) // CHUNK
    mask = pos_chunk[None, :] <= pos_chunk[:, None]
    ws_m = jnp.where(mask[None], ws, jnp.zeros_like(ws))
    vb = v.reshape(B, n_blk, GMLP_BLOCK, GMLP_GROUPS, GMLP_GROUP_DIM)
    mixed = jnp.einsum('gts,bnsgc->bntgc', ws_m, vb) + bs.T[None, None, :, :, None]
    return u * mixed.reshape(B, S, GMLP_WIDTH)


def dsa_attention(q, k, v, q_idx, k_idx, w_idx):
    B, S, H, hd = q.shape
    n_blk = S // Q_BLOCK
    top = min(TOPK_MAX, S // 4)
    key_chunk = jnp.arange(S) // CHUNK
    gather = jax.vmap(lambda tab, ids: tab[ids])

    def one_block(args):
        qb, qib, wb, blk = args
        q_chunk = (blk * Q_BLOCK + jnp.arange(Q_BLOCK)) // CHUNK
        rel = jax.nn.relu(jnp.einsum('bthd,bsd->bths', qib, k_idx).astype(jnp.float32))
        score = jnp.einsum('bth,bths->bts', wb.astype(jnp.float32), rel)
        adm = key_chunk[None, :] <= q_chunk[:, None]
        score = jnp.where(adm[None], score, -jnp.inf)
        _, sel = lax.top_k(score, top)
        valid = (sel // CHUNK) <= q_chunk[None, :, None]
        k_sel = gather(k, sel)
        v_sel = gather(v, sel)
        logits = jnp.einsum('bthd,btkd->bthk', qb, k_sel).astype(jnp.float32) * (hd ** -0.5)
        logits = jnp.where(valid[:, :, None, :], logits, -jnp.inf)
        p = jax.nn.softmax(logits, axis=-1).astype(v.dtype)
        return jnp.einsum('bthk,btkd->bthd', p, v_sel)

    to_blocks = lambda a: a.reshape(B, n_blk, Q_BLOCK, *a.shape[2:]).swapaxes(0, 1)
    out = lax.map(one_block, (to_blocks(q), to_blocks(q_idx), to_blocks(w_idx), jnp.arange(n_blk)))
    return out.swapaxes(0, 1).reshape(B, S, H * hd)


def setup_inputs(seed: int = 0) -> dict:
    key = jax.random.key(seed)
    ks = jax.random.split(key, 24)
    L = DEPTH
    nrm = lambda k, shape, s: jax.random.normal(k, shape, jnp.float32) * s
    gain = lambda k, n: 1.0 + nrm(k, (L, n), 0.02)
    return {
        "x": nrm(ks[0], (BATCH, SEQ, D_MODEL), 1.0),
        "ffn1_norm": gain(ks[1], D_MODEL),
        "ffn1_w1": nrm(ks[2], (L, D_MODEL, D_FF), D_MODEL ** -0.5),
        "ffn1_w3": nrm(ks[3], (L, D_MODEL, D_FF), D_MODEL ** -0.5),
        "ffn1_w2": nrm(ks[4], (L, D_FF, D_MODEL), D_FF ** -0.5),
        "mix_norm": gain(ks[5], D_MODEL),
        "w_in": nrm(ks[6], (L, D_MODEL, IN_WIDTH), D_MODEL ** -0.5),
        "gate_bias": nrm(ks[7], (L, 2 * D_MODEL), 0.02),
        "gmlp_v_norm": gain(ks[8], GMLP_WIDTH),
        "gmlp_ws": nrm(ks[9], (L, GMLP_GROUPS, GMLP_BLOCK, GMLP_BLOCK), GMLP_BLOCK ** -0.5),
        "gmlp_bs": 1.0 + nrm(ks[10], (L, GMLP_GROUPS, GMLP_BLOCK), 0.02),
        "q_norm": gain(ks[11], HEAD_DIM),
        "k_norm": gain(ks[12], HEAD_DIM),
        "idx_k_norm": gain(ks[13], IDX_DIM),
        "w_br_a": nrm(ks[14], (L, GMLP_WIDTH, D_MODEL), GMLP_WIDTH ** -0.5),
        "w_br_b": nrm(ks[15], (L, ATTN_WIDTH, D_MODEL), ATTN_WIDTH ** -0.5),
        "w_out": nrm(ks[16], (L, D_MODEL, D_MODEL), D_MODEL ** -0.5),
        "ffn2_norm": gain(ks[17], D_MODEL),
        "ffn2_w1": nrm(ks[18], (L, D_MODEL, D_FF), D_MODEL ** -0.5),
        "ffn2_w3": nrm(ks[19], (L, D_MODEL, D_FF), D_MODEL ** -0.5),
        "ffn2_w2": nrm(ks[20], (L, D_FF, D_MODEL), D_FF ** -0.5),
    }


def reference(x, ffn1_norm, ffn1_w1, ffn1_w3, ffn1_w2, mix_norm, w_in, gate_bias,
              gmlp_v_norm, gmlp_ws, gmlp_bs, q_norm, k_norm, idx_k_norm,
              w_br_a, w_br_b, w_out, ffn2_norm, ffn2_w1, ffn2_w3, ffn2_w2):
    B, S, _ = x.shape
    cos_a, sin_a = rope_tables(S, HEAD_DIM)
    cos_i, sin_i = rope_tables(S, IDX_DIM)
    split_at = [int(c) for c in np.cumsum(SPLITS)[:-1]]
    idx_scale = (IDX_HEADS ** -0.5) * (IDX_DIM ** -0.5)
    h = x
    for l in range(DEPTH):
        h = h + 0.5 * swiglu(rms_norm(h, ffn1_norm[l]), ffn1_w1[l], ffn1_w3[l], ffn1_w2[l])
        n = rms_norm(h, mix_norm[l])
        z = n @ w_in[l]
        (u_a, v_a, q, k, v, q_i, k_i, w_i, g_a, g_b) = jnp.split(z, split_at, axis=-1)
        y_a = gmlp_spatial_gating(u_a, v_a, gmlp_v_norm[l], gmlp_ws[l], gmlp_bs[l])
        q = apply_rope(rms_norm(q.reshape(B, S, ATTN_HEADS, HEAD_DIM), q_norm[l]), cos_a, sin_a)
        k = apply_rope(rms_norm(k, k_norm[l]), cos_a, sin_a)
        q_i = apply_rope(q_i.reshape(B, S, IDX_HEADS, IDX_DIM), cos_i, sin_i)
        k_i = apply_rope(rms_norm(k_i, idx_k_norm[l]), cos_i, sin_i)
        y_b = dsa_attention(q, k, v, q_i, k_i, w_i * idx_scale)
        gates = jax.nn.sigmoid((jnp.concatenate([g_a, g_b], axis=-1) + gate_bias[l]).astype(jnp.float32)).astype(h.dtype)
        gate_a, gate_b = jnp.split(gates, 2, axis=-1)
        m = gate_a * (y_a @ w_br_a[l]) + gate_b * (y_b @ w_br_b[l])
        h = h + m @ w_out[l]
        h = h + 0.5 * swiglu(rms_norm(h, ffn2_norm[l]), ffn2_w1[l], ffn2_w3[l], ffn2_w2[l])
    return h
```

```python
import functools
import math

import jax
import jax.numpy as jnp
import numpy as np
from jax import lax
from jax.experimental import pallas as pl
from jax.experimental.pallas import tpu as pltpu

CHUNK = 64
GROUP_DIM = 128
MIX_BLOCK = 128
HEAD_DIM = 128
IDX_DIM = 64
Q_BLOCK = 128
KEY_TILE = 128
TOPK_MAX = 256
ROPE_THETA = 10000.0
EPS = 1e-6
LANES = 128
NEG_BIG = -1e30
INT_MIN = -(2 ** 31)
VMEM_LIMIT_BYTES = 60000 * 1024

f32 = jnp.float32
bf16 = jnp.bfloat16


def _params(sem):
    return pltpu.CompilerParams(dimension_semantics=sem,
                                vmem_limit_bytes=VMEM_LIMIT_BYTES)


def _tile(dim, want):
    return dim if dim <= want else want


def _gelu(x):
    return 0.5 * x * (1.0 + lax.erf(x * (2.0 ** -0.5)))


def _rope_pairs(x, cos_t, sin_t, half):
    width = x.shape[-1]
    lane = lax.broadcasted_iota(jnp.int32, x.shape, x.ndim - 1)
    fwd = pltpu.roll(x, width - half, axis=x.ndim - 1)
    bwd = pltpu.roll(x, half, axis=x.ndim - 1)
    partner = jnp.where((lane % (2 * half)) < half, fwd, bwd)
    return x * cos_t + partner * sin_t


def _rmsnorm_kernel(x_ref, g_ref, o_ref):
    x = x_ref[...]
    ms = jnp.mean(x * x, axis=-1, keepdims=True)
    o_ref[...] = (x * lax.rsqrt(ms + EPS) * g_ref[...]).astype(o_ref.dtype)


def rmsnorm_bf16(x, gain, tm=256):
    n, d = x.shape
    tm = _tile(n, tm)
    return pl.pallas_call(
        _rmsnorm_kernel,
        grid=(pl.cdiv(n, tm),),
        in_specs=[pl.BlockSpec((tm, d), lambda i: (i, 0)),
                  pl.BlockSpec((1, d), lambda i: (0, 0))],
        out_specs=pl.BlockSpec((tm, d), lambda i: (i, 0)),
        out_shape=jax.ShapeDtypeStruct((n, d), bf16),
        compiler_params=_params(("parallel",)),
        name="rmsnorm",
    )(x, gain.reshape(1, d))


def _swiglu_up_kernel(x_ref, w1_ref, w3_ref, o_ref):
    x = x_ref[...]
    a = jnp.dot(x, w1_ref[...], preferred_element_type=f32)
    b = jnp.dot(x, w3_ref[...], preferred_element_type=f32)
    o_ref[...] = (a * jax.nn.sigmoid(a) * b).astype(o_ref.dtype)


def swiglu_up(xn, w1, w3, tm=1024, tn=512):
    n, d = xn.shape
    f = w1.shape[1]
    tm, tn = _tile(n, tm), _tile(f, tn)
    return pl.pallas_call(
        _swiglu_up_kernel,
        grid=(pl.cdiv(n, tm), pl.cdiv(f, tn)),
        in_specs=[pl.BlockSpec((tm, d), lambda i, j: (i, 0)),
                  pl.BlockSpec((d, tn), lambda i, j: (0, j)),
                  pl.BlockSpec((d, tn), lambda i, j: (0, j))],
        out_specs=pl.BlockSpec((tm, tn), lambda i, j: (i, j)),
        out_shape=jax.ShapeDtypeStruct((n, f), bf16),
        compiler_params=_params(("parallel", "arbitrary")),
        name="swiglu_up",
    )(xn, w1, w3)


def _mm_kernel(*refs, nk, n_extra, epilogue):
    a_ref, w_ref = refs[0], refs[1]
    extra = refs[2:2 + n_extra]
    o_ref = refs[2 + n_extra]
    part = jnp.dot(a_ref[...], w_ref[...], preferred_element_type=f32)
    if nk == 1:
        o_ref[...] = epilogue(part, *[e[...] for e in extra]).astype(o_ref.dtype)
        return
    acc_ref = refs[3 + n_extra]
    k = pl.program_id(2)

    @pl.when(k == 0)
    def _():
        acc_ref[...] = part

    @pl.when(k > 0)
    def _():
        acc_ref[...] += part

    @pl.when(k == nk - 1)
    def _():
        o_ref[...] = epilogue(acc_ref[...], *[e[...] for e in extra]).astype(o_ref.dtype)


def matmul(a, w, epilogue, out_dtype, extras=(), tm=1024, tn=512, nk=1, name="mm"):
    m, kdim = a.shape
    n = w.shape[1]
    tm, tn = _tile(m, tm), _tile(n, tn)
    assert kdim % nk == 0
    tk = kdim // nk
    assert nk == 1 or tk % LANES == 0
    in_specs = [pl.BlockSpec((tm, tk), lambda i, j, k: (i, k)),
                pl.BlockSpec((tk, tn), lambda i, j, k: (k, j))]
    arrays = []
    for arr, kind in extras:
        arrays.append(arr)
        if kind == "tile":
            in_specs.append(pl.BlockSpec((tm, tn), lambda i, j, k: (i, j)))
        elif kind == "row":
            in_specs.append(pl.BlockSpec((1, tn), lambda i, j, k: (0, j)))
        elif kind == "const":
            in_specs.append(pl.BlockSpec(arr.shape, lambda i, j, k: (0, 0)))
        elif kind == "pos":
            assert arr.shape[0] % tm == 0
            nper = arr.shape[0] // tm
            in_specs.append(pl.BlockSpec((tm, arr.shape[1]),
                                         lambda i, j, k, nper=nper: (i % nper, 0)))
        else:
            raise ValueError(kind)
    scratch = [pltpu.VMEM((tm, tn), f32)] if nk > 1 else []
    return pl.pallas_call(
        functools.partial(_mm_kernel, nk=nk, n_extra=len(arrays), epilogue=epilogue),
        grid=(pl.cdiv(m, tm), pl.cdiv(n, tn), nk),
        in_specs=in_specs,
        out_specs=pl.BlockSpec((tm, tn), lambda i, j, k: (i, j)),
        out_shape=jax.ShapeDtypeStruct((m, n), out_dtype),
        scratch_shapes=scratch,
        compiler_params=_params(("parallel", "arbitrary", "arbitrary")),
        name=name,
    )(a, w, *arrays)


def _ep_half_residual(acc, res):
    return res + 0.5 * acc


def _ep_residual(acc, res):
    return res + acc


def _ep_gelu(acc):
    return _gelu(acc)


def _ep_gate(acc, bias):
    return jax.nn.sigmoid(acc + bias)


def _ep_q(acc, gain, cos_t, sin_t, *, scale):
    outs = []
    for h in range(acc.shape[1] // HEAD_DIM):
        xh = acc[:, h * HEAD_DIM:(h + 1) * HEAD_DIM]
        ms = jnp.mean(xh * xh, axis=-1, keepdims=True)
        xh = xh * lax.rsqrt(ms + EPS) * gain
        outs.append(_rope_pairs(xh, cos_t, sin_t, HEAD_DIM // 2) * scale)
    return jnp.concatenate(outs, axis=-1)


def _ep_kv(acc, gain, cos_t, sin_t):
    k = acc[:, :HEAD_DIM]
    ms = jnp.mean(k * k, axis=-1, keepdims=True)
    k = _rope_pairs(k * lax.rsqrt(ms + EPS) * gain, cos_t, sin_t, HEAD_DIM // 2)
    return jnp.concatenate([k, acc[:, HEAD_DIM:]], axis=-1)


def _ep_qi(acc, cos_t, sin_t):
    outs = []
    for c in range(acc.shape[1] // LANES):
        outs.append(_rope_pairs(acc[:, c * LANES:(c + 1) * LANES], cos_t, sin_t, IDX_DIM // 2))
    return jnp.concatenate(outs, axis=-1)


def _ep_small(acc, gain, cos_t, sin_t, *, idx_scale):
    lane = lax.broadcasted_iota(jnp.int32, acc.shape, 1)
    is_k = lane < IDX_DIM
    ms = jnp.sum(jnp.where(is_k, acc * acc, 0.0), axis=-1, keepdims=True) * (1.0 / IDX_DIM)
    k = _rope_pairs(acc * lax.rsqrt(ms + EPS) * gain, cos_t, sin_t, IDX_DIM // 2)
    return jnp.where(is_k, k, acc * idx_scale)


def _gmlp_kernel(u_ref, v_ref, gain_ref, ws_ref, bs_ref, o_ref):
    groups = ws_ref.shape[0]
    v = v_ref[...].astype(f32)
    ms = jnp.mean(v * v, axis=-1, keepdims=True)
    vn = (v * lax.rsqrt(ms + EPS) * gain_ref[...]).astype(bf16)
    row_chunk = lax.broadcasted_iota(jnp.int32, (MIX_BLOCK, MIX_BLOCK), 0) // CHUNK
    col_chunk = lax.broadcasted_iota(jnp.int32, (MIX_BLOCK, MIX_BLOCK), 1) // CHUNK
    causal = col_chunk <= row_chunk
    bs = bs_ref[...]
    for g in range(groups):
        sl = slice(g * GROUP_DIM, (g + 1) * GROUP_DIM)
        w = jnp.where(causal, ws_ref[g], 0.0).astype(bf16)
        mixed = jnp.dot(w, vn[:, sl], preferred_element_type=f32) + bs[:, g:g + 1]
        o_ref[:, sl] = (u_ref[:, sl].astype(f32) * mixed).astype(o_ref.dtype)


def gmlp_gate(uv, gain, ws, bs_t):
    n = uv.shape[0]
    gw = uv.shape[1] // 2
    groups = gw // GROUP_DIM
    return pl.pallas_call(
        _gmlp_kernel,
        grid=(n // MIX_BLOCK,),
        in_specs=[pl.BlockSpec((MIX_BLOCK, gw), lambda i: (i, 0)),
                  pl.BlockSpec((MIX_BLOCK, gw), lambda i: (i, 1)),
                  pl.BlockSpec((1, gw), lambda i: (0, 0)),
                  pl.BlockSpec((groups, MIX_BLOCK, MIX_BLOCK), lambda i: (0, 0, 0)),
                  pl.BlockSpec((MIX_BLOCK, groups), lambda i: (0, 0))],
        out_specs=pl.BlockSpec((MIX_BLOCK, gw), lambda i: (i, 0)),
        out_shape=jax.ShapeDtypeStruct((n, gw), bf16),
        compiler_params=_params(("parallel",)),
        name="gmlp_gate",
    )(uv, uv, gain.reshape(1, gw), ws, bs_t)


def _sortable(x):
    bits = lax.bitcast_convert_type(x, jnp.int32)
    return bits ^ ((bits >> 31) & jnp.int32(0x7FFFFFFF))


def _dsa_kernel(q_ref, kv_ref, qi_ref, smallk_ref, smallq_ref, o_ref,
                qit_ref, q2_ref, keys_ref, bias_ref, m_ref, l_ref, acc_ref,
                *, n_qblk, top):
    qb = pl.program_id(0) % n_qblk
    n_kt = qb + 1
    heads = q_ref.shape[1] // HEAD_DIM
    idx_heads = qi_ref.shape[1] // IDX_DIM

    for c in range(qi_ref.shape[1] // LANES):
        t = jnp.transpose(qi_ref[:, c * LANES:(c + 1) * LANES].astype(f32)).astype(bf16)
        qit_ref[:, (2 * c) * Q_BLOCK:(2 * c + 1) * Q_BLOCK] = t[:IDX_DIM]
        qit_ref[:, (2 * c + 1) * Q_BLOCK:(2 * c + 2) * Q_BLOCK] = t[IDX_DIM:]
    for h in range(heads):
        q2_ref[h * Q_BLOCK:(h + 1) * Q_BLOCK, :] = q_ref[:, h * HEAD_DIM:(h + 1) * HEAD_DIM]
    w_t = jnp.transpose(smallq_ref[...])
    q_chunk = (qb * Q_BLOCK + lax.broadcasted_iota(jnp.int32, (KEY_TILE, Q_BLOCK), 1)) // CHUNK
    key_off = lax.broadcasted_iota(jnp.int32, (KEY_TILE, Q_BLOCK), 0)

    def score_tile(kt, carry):
        base = pl.multiple_of(kt * KEY_TILE, KEY_TILE)
        ki = smallk_ref[pl.ds(base, KEY_TILE), :][:, :IDX_DIM].astype(bf16)
        rel = jnp.dot(ki, qit_ref[...], preferred_element_type=f32)
        s = jnp.zeros((KEY_TILE, Q_BLOCK), f32)
        for h in range(idx_heads):
            r = jnp.maximum(rel[:, h * Q_BLOCK:(h + 1) * Q_BLOCK], 0.0)
            s = s + w_t[IDX_DIM + h:IDX_DIM + h + 1, :] * r
        adm = (base + key_off) // CHUNK <= q_chunk
        keys_ref[kt] = jnp.where(adm, _sortable(s), INT_MIN)
        return carry

    lax.fori_loop(0, n_kt, score_tile, 0)

    def count_keys(pred):
        def count_tile(kt, cnt):
            hit = pred(keys_ref[kt], kt * KEY_TILE + key_off).astype(jnp.int32)
            return cnt + jnp.sum(hit.reshape(KEY_TILE // 8, 8, Q_BLOCK), axis=0)

        cnt = lax.fori_loop(0, n_kt, count_tile, jnp.zeros((8, Q_BLOCK), jnp.int32))
        return jnp.sum(cnt, axis=0, keepdims=True)

    def bit_step(i, carry):
        prefix, n_ge = carry
        cand = prefix + jnp.left_shift(jnp.int32(1), 31 - i)
        total = count_keys(lambda kk, idx: kk >= cand)
        take = total >= top
        return jnp.where(take, cand, prefix), jnp.where(take, total, n_ge)

    thr, n_ge = lax.fori_loop(
        0, 32, bit_step,
        (jnp.full((1, Q_BLOCK), INT_MIN, jnp.int32), jnp.zeros((1, Q_BLOCK), jnp.int32)))

    def tie_search(_):
        def idx_step(i, bound):
            cand = bound + jnp.left_shift(jnp.int32(1), idx_bits - 1 - i)
            total = count_keys(lambda kk, idx: (kk > thr) | ((kk == thr) & (idx < cand)))
            return jnp.where(total < top, cand, bound)

        return lax.fori_loop(0, idx_bits, idx_step, jnp.zeros((1, Q_BLOCK), jnp.int32))

    idx_bits = int(keys_ref.shape[0] * KEY_TILE).bit_length()
    no_ties = lambda _: jnp.full((1, Q_BLOCK), keys_ref.shape[0] * KEY_TILE, jnp.int32)
    last_idx = lax.cond(jnp.max(n_ge) > top, tie_search, no_ties, 0)

    def bias_tile(kt, carry):
        kk = keys_ref[kt]
        idx = kt * KEY_TILE + key_off
        sel = ((kk > thr) | ((kk == thr) & (idx <= last_idx))) & (kk != INT_MIN)
        bias_ref[kt] = jnp.transpose(jnp.where(sel, 0.0, NEG_BIG))
        return carry

    lax.fori_loop(0, n_kt, bias_tile, 0)

    m_ref[...] = jnp.full(m_ref.shape, NEG_BIG, f32)
    l_ref[...] = jnp.zeros(l_ref.shape, f32)
    acc_ref[...] = jnp.zeros(acc_ref.shape, f32)

    def attn_tile(kt, carry):
        base = pl.multiple_of(kt * KEY_TILE, KEY_TILE)
        kvt = kv_ref[pl.ds(base, KEY_TILE), :]
        k_t, v_t = kvt[:, :HEAD_DIM], kvt[:, HEAD_DIM:]
        logits = lax.dot_general(q2_ref[...], k_t, (((1,), (1,)), ((), ())),
                                 preferred_element_type=f32)
        logits = (logits.reshape(heads, Q_BLOCK, KEY_TILE) + bias_ref[kt][None]
                  ).reshape(heads * Q_BLOCK, KEY_TILE)
        m_old = m_ref[...]
        m_new = jnp.maximum(m_old, jnp.max(logits, axis=-1, keepdims=True))
        alpha = jnp.exp(m_old - m_new)
        p = jnp.exp(logits - m_new)
        l_ref[...] = alpha * l_ref[...] + jnp.sum(p, axis=-1, keepdims=True)
        acc_ref[...] = alpha * acc_ref[...] + jnp.dot(p.astype(bf16), v_t,
                                                      preferred_element_type=f32)
        m_ref[...] = m_new
        return carry

    lax.fori_loop(0, n_kt, attn_tile, 0)

    out = acc_ref[...] / l_ref[...]
    for h in range(heads):
        o_ref[:, h * HEAD_DIM:(h + 1) * HEAD_DIM] = (
            out[h * Q_BLOCK:(h + 1) * Q_BLOCK, :].astype(o_ref.dtype))


def dsa_attention(q, kv, qi, small, batch, seq):
    n, aw = q.shape
    n_qblk = seq // Q_BLOCK
    n_kt = seq // KEY_TILE
    heads = aw // HEAD_DIM
    iw = qi.shape[1]
    top = min(TOPK_MAX, seq // 4)
    return pl.pallas_call(
        functools.partial(_dsa_kernel, n_qblk=n_qblk, top=top),
        grid=(batch * n_qblk,),
        in_specs=[pl.BlockSpec((Q_BLOCK, aw), lambda g: (g, 0)),
                  pl.BlockSpec((seq, 2 * HEAD_DIM), lambda g: (g // n_qblk, 0)),
                  pl.BlockSpec((Q_BLOCK, iw), lambda g: (g, 0)),
                  pl.BlockSpec((seq, LANES), lambda g: (g // n_qblk, 0)),
                  pl.BlockSpec((Q_BLOCK, LANES), lambda g: (g, 0))],
        out_specs=pl.BlockSpec((Q_BLOCK, aw), lambda g: (g, 0)),
        out_shape=jax.ShapeDtypeStruct((n, aw), bf16),
        scratch_shapes=[pltpu.VMEM((IDX_DIM, (iw // IDX_DIM) * Q_BLOCK), bf16),
                        pltpu.VMEM((heads * Q_BLOCK, HEAD_DIM), bf16),
                        pltpu.VMEM((n_kt, KEY_TILE, Q_BLOCK), jnp.int32),
                        pltpu.VMEM((n_kt, Q_BLOCK, KEY_TILE), f32),
                        pltpu.VMEM((heads * Q_BLOCK, 1), f32),
                        pltpu.VMEM((heads * Q_BLOCK, 1), f32),
                        pltpu.VMEM((heads * Q_BLOCK, HEAD_DIM), f32)],
        compiler_params=_params(("arbitrary",)),
        name="dsa_attention",
    )(q, kv, qi, small, small)


def _merge_kernel(ya_ref, yb_ref, wa_ref, wb_ref, ga_ref, gb_ref, o_ref):
    a = jnp.dot(ya_ref[...], wa_ref[...], preferred_element_type=f32)
    b = jnp.dot(yb_ref[...], wb_ref[...], preferred_element_type=f32)
    o_ref[...] = (ga_ref[...].astype(f32) * a + gb_ref[...].astype(f32) * b).astype(o_ref.dtype)


def gated_merge(ya, yb, wa, wb, gates, tm=1024, tn=512):
    n, ka = ya.shape
    kb = yb.shape[1]
    d = wa.shape[1]
    tm, tn = _tile(n, tm), _tile(d, tn)
    nj = d // tn
    assert d % tn == 0
    return pl.pallas_call(
        _merge_kernel,
        grid=(pl.cdiv(n, tm), nj),
        in_specs=[pl.BlockSpec((tm, ka), lambda i, j: (i, 0)),
                  pl.BlockSpec((tm, kb), lambda i, j: (i, 0)),
                  pl.BlockSpec((ka, tn), lambda i, j: (0, j)),
                  pl.BlockSpec((kb, tn), lambda i, j: (0, j)),
                  pl.BlockSpec((tm, tn), lambda i, j: (i, j)),
                  pl.BlockSpec((tm, tn), lambda i, j: (i, j + nj))],
        out_specs=pl.BlockSpec((tm, tn), lambda i, j: (i, j)),
        out_shape=jax.ShapeDtypeStruct((n, d), bf16),
        compiler_params=_params(("parallel", "arbitrary")),
        name="gated_merge",
    )(ya, yb, wa, wb, gates, gates)


def _rope_tables(seq, dim, lanes):
    inv = ROPE_THETA ** (-jnp.arange(0, dim, 2, dtype=f32) / dim)
    ang = jnp.arange(seq, dtype=f32)[:, None] * inv[None, :]
    cos, sin = jnp.cos(ang), jnp.sin(ang)
    reps = lanes // dim
    cos_t = jnp.tile(jnp.concatenate([cos, cos], axis=-1), (1, reps))
    sin_t = jnp.tile(jnp.concatenate([-sin, sin], axis=-1), (1, reps))
    return cos_t, sin_t


def _ffn(h, gain, w1, w3, w2, nk_down):
    xn = rmsnorm_bf16(h, gain)
    g = swiglu_up(xn, w1.astype(bf16), w3.astype(bf16))
    return matmul(g, w2.astype(bf16), _ep_half_residual, f32, extras=[(h, "tile")],
                  nk=nk_down, name="ffn_down")


def kernel(x, ffn1_norm, ffn1_w1, ffn1_w3, ffn1_w2, mix_norm, w_in, gate_bias,
           gmlp_v_norm, gmlp_ws, gmlp_bs, q_norm, k_norm, idx_k_norm,
           w_br_a, w_br_b, w_out, ffn2_norm, ffn2_w1, ffn2_w3, ffn2_w2):
    batch, seq, d = x.shape
    depth = ffn1_norm.shape[0]
    n = batch * seq
    gw = gmlp_v_norm.shape[1]
    aw = w_br_b.shape[1]
    idx_heads = d // 128
    iw = idx_heads * IDX_DIM
    d_ff = ffn1_w1.shape[2]
    nk_down = 2 if (d_ff % (2 * LANES) == 0 and d_ff > 4096) else 1
    c_uv, c_q, c_kv, c_qi = 2 * gw, aw, 2 * HEAD_DIM, iw
    o_q = c_uv
    o_kv = o_q + c_q
    o_qi = o_kv + c_kv
    o_small = o_qi + c_qi
    o_gate = o_small + IDX_DIM + idx_heads
    assert o_gate + 2 * d == w_in.shape[2]
    assert IDX_DIM + idx_heads <= LANES
    idx_scale = (idx_heads ** -0.5) * (IDX_DIM ** -0.5)

    cos_a, sin_a = _rope_tables(seq, HEAD_DIM, LANES)
    cos_i, sin_i = _rope_tables(seq, IDX_DIM, LANES)
    tm = _tile(seq, 1024)
    assert seq % tm == 0 and seq % Q_BLOCK == 0

    h = x.reshape(n, d)
    for l in range(depth):
        h = _ffn(h, ffn1_norm[l], ffn1_w1[l], ffn1_w3[l], ffn1_w2[l], nk_down)

        nrm = rmsnorm_bf16(h, mix_norm[l])
        wl = w_in[l]
        w_uv = wl[:, :c_uv].astype(bf16)
        w_q = wl[:, o_q:o_q + c_q].astype(bf16)
        w_kv = wl[:, o_kv:o_kv + c_kv].astype(bf16)
        w_qi = wl[:, o_qi:o_qi + c_qi].astype(bf16)
        w_small = jnp.pad(wl[:, o_small:o_gate], ((0, 0), (0, LANES - (o_gate - o_small)))).astype(bf16)
        w_gate = wl[:, o_gate:].astype(bf16)

        uv = matmul(nrm, w_uv, _ep_gelu, bf16, tm=tm, name="proj_uv")
        q = matmul(nrm, w_q, functools.partial(_ep_q, scale=HEAD_DIM ** -0.5), bf16,
                   extras=[(q_norm[l].reshape(1, HEAD_DIM), "const"),
                           (cos_a, "pos"), (sin_a, "pos")], tm=tm, name="proj_q")
        kv = matmul(nrm, w_kv, _ep_kv, bf16,
                    extras=[(k_norm[l].reshape(1, HEAD_DIM), "const"),
                            (cos_a, "pos"), (sin_a, "pos")], tm=tm, name="proj_kv")
        qi = matmul(nrm, w_qi, _ep_qi, bf16,
                    extras=[(cos_i, "pos"), (sin_i, "pos")], tm=tm, name="proj_qi")
        k_gain = jnp.tile(idx_k_norm[l].reshape(1, IDX_DIM), (1, LANES // IDX_DIM))
        small = matmul(nrm, w_small, functools.partial(_ep_small, idx_scale=idx_scale), f32,
                       extras=[(k_gain, "const"), (cos_i, "pos"), (sin_i, "pos")],
                       tm=tm, name="proj_small")
        gates = matmul(nrm, w_gate, _ep_gate, bf16,
                       extras=[(gate_bias[l].reshape(1, 2 * d), "row")], tm=tm, name="proj_gate")

        ya = gmlp_gate(uv, gmlp_v_norm[l], gmlp_ws[l], gmlp_bs[l].T)
        yb = dsa_attention(q, kv, qi, small, batch, seq)
        m = gated_merge(ya, yb, w_br_a[l].astype(bf16), w_br_b[l].astype(bf16), gates, tm=tm)
        h = matmul(m, w_out[l].astype(bf16), _ep_residual, f32, extras=[(h, "tile")],
                   tm=tm, name="out_proj")

        h = _ffn(h, ffn2_norm[l], ffn2_w1[l], ffn2_w3[l], ffn2_w2[l], nk_down)
    return h.reshape(batch, seq, d)
```

```python
import functools
import math

import jax
import jax.numpy as jnp
import numpy as np
from jax import lax
from jax.experimental import pallas as pl
from jax.experimental.pallas import tpu as pltpu

CHUNK = 64
GROUP_DIM = 128
MIX_BLOCK = 128
HEAD_DIM = 128
IDX_DIM = 64
Q_BLOCK = 128
KEY_TILE = 128
SELECT_UNROLL = 4
TOPK_MAX = 256
ROPE_THETA = 10000.0
EPS = 1e-6
LANES = 128
NEG_BIG = -1e30
INT_MIN = -(2 ** 31)
VMEM_LIMIT_BYTES = 60000 * 1024

f32 = jnp.float32
bf16 = jnp.bfloat16


def _params(sem):
    return pltpu.CompilerParams(dimension_semantics=sem,
                                vmem_limit_bytes=VMEM_LIMIT_BYTES)


def _tile(dim, want):
    return dim if dim <= want else want


def _gelu(x):
    return 0.5 * x * (1.0 + lax.erf(x * (2.0 ** -0.5)))


def _rope_pairs(x, cos_t, sin_t, half):
    width = x.shape[-1]
    lane = lax.broadcasted_iota(jnp.int32, x.shape, x.ndim - 1)
    fwd = pltpu.roll(x, width - half, axis=x.ndim - 1)
    bwd = pltpu.roll(x, half, axis=x.ndim - 1)
    partner = jnp.where((lane % (2 * half)) < half, fwd, bwd)
    return x * cos_t + partner * sin_t


def _rmsnorm_kernel(x_ref, g_ref, o_ref):
    x = x_ref[...]
    ms = jnp.mean(x * x, axis=-1, keepdims=True)
    o_ref[...] = (x * lax.rsqrt(ms + EPS) * g_ref[...]).astype(o_ref.dtype)


def rmsnorm_bf16(x, gain, tm=256):
    n, d = x.shape
    tm = _tile(n, tm)
    return pl.pallas_call(
        _rmsnorm_kernel,
        grid=(pl.cdiv(n, tm),),
        in_specs=[pl.BlockSpec((tm, d), lambda i: (i, 0)),
                  pl.BlockSpec((1, d), lambda i: (0, 0))],
        out_specs=pl.BlockSpec((tm, d), lambda i: (i, 0)),
        out_shape=jax.ShapeDtypeStruct((n, d), bf16),
        compiler_params=_params(("parallel",)),
        name="rmsnorm",
    )(x, gain.reshape(1, d))


def _swiglu_up_kernel(x_ref, w1_ref, w3_ref, o_ref):
    x = x_ref[...]
    a = jnp.dot(x, w1_ref[...], preferred_element_type=f32)
    b = jnp.dot(x, w3_ref[...], preferred_element_type=f32)
    o_ref[...] = (a * jax.nn.sigmoid(a) * b).astype(o_ref.dtype)


def swiglu_up(xn, w1, w3, tm=1024, tn=512):
    n, d = xn.shape
    f = w1.shape[1]
    tm, tn = _tile(n, tm), _tile(f, tn)
    return pl.pallas_call(
        _swiglu_up_kernel,
        grid=(pl.cdiv(n, tm), pl.cdiv(f, tn)),
        in_specs=[pl.BlockSpec((tm, d), lambda i, j: (i, 0)),
                  pl.BlockSpec((d, tn), lambda i, j: (0, j)),
                  pl.BlockSpec((d, tn), lambda i, j: (0, j))],
        out_specs=pl.BlockSpec((tm, tn), lambda i, j: (i, j)),
        out_shape=jax.ShapeDtypeStruct((n, f), bf16),
        compiler_params=_params(("parallel", "arbitrary")),
        name="swiglu_up",
    )(xn, w1, w3)


def _mm_kernel(*refs, nk, n_extra, epilogue):
    a_ref, w_ref = refs[0], refs[1]
    extra = refs[2:2 + n_extra]
    o_ref = refs[2 + n_extra]
    part = jnp.dot(a_ref[...], w_ref[...], preferred_element_type=f32)
    if nk == 1:
        o_ref[...] = epilogue(part, *[e[...] for e in extra]).astype(o_ref.dtype)
        return
    acc_ref = refs[3 + n_extra]
    k = pl.program_id(2)

    @pl.when(k == 0)
    def _():
        acc_ref[...] = part

    @pl.when(k > 0)
    def _():
        acc_ref[...] += part

    @pl.when(k == nk - 1)
    def _():
        o_ref[...] = epilogue(acc_ref[...], *[e[...] for e in extra]).astype(o_ref.dtype)


def matmul(a, w, epilogue, out_dtype, extras=(), tm=1024, tn=512, nk=1, name="mm"):
    m, kdim = a.shape
    n = w.shape[1]
    tm, tn = _tile(m, tm), _tile(n, tn)
    assert kdim % nk == 0
    tk = kdim // nk
    assert nk == 1 or tk % LANES == 0
    in_specs = [pl.BlockSpec((tm, tk), lambda i, j, k: (i, k)),
                pl.BlockSpec((tk, tn), lambda i, j, k: (k, j))]
    arrays = []
    for arr, kind in extras:
        arrays.append(arr)
        if kind == "tile":
            in_specs.append(pl.BlockSpec((tm, tn), lambda i, j, k: (i, j)))
        elif kind == "row":
            in_specs.append(pl.BlockSpec((1, tn), lambda i, j, k: (0, j)))
        elif kind == "const":
            in_specs.append(pl.BlockSpec(arr.shape, lambda i, j, k: (0, 0)))
        elif kind == "pos":
            assert arr.shape[0] % tm == 0
            nper = arr.shape[0] // tm
            in_specs.append(pl.BlockSpec((tm, arr.shape[1]),
                                         lambda i, j, k, nper=nper: (i % nper, 0)))
        else:
            raise ValueError(kind)
    scratch = [pltpu.VMEM((tm, tn), f32)] if nk > 1 else []
    return pl.pallas_call(
        functools.partial(_mm_kernel, nk=nk, n_extra=len(arrays), epilogue=epilogue),
        grid=(pl.cdiv(m, tm), pl.cdiv(n, tn), nk),
        in_specs=in_specs,
        out_specs=pl.BlockSpec((tm, tn), lambda i, j, k: (i, j)),
        out_shape=jax.ShapeDtypeStruct((m, n), out_dtype),
        scratch_shapes=scratch,
        compiler_params=_params(("parallel", "arbitrary", "arbitrary")),
        name=name,
    )(a, w, *arrays)


def _ep_half_residual(acc, res):
    return res + 0.5 * acc


def _ep_residual(acc, res):
    return res + acc


def _ep_gelu(acc):
    return _gelu(acc)


def _ep_gate(acc, bias):
    return jax.nn.sigmoid(acc + bias)


def _ep_q(acc, gain, cos_t, sin_t, *, scale):
    outs = []
    for h in range(acc.shape[1] // HEAD_DIM):
        xh = acc[:, h * HEAD_DIM:(h + 1) * HEAD_DIM]
        ms = jnp.mean(xh * xh, axis=-1, keepdims=True)
        xh = xh * lax.rsqrt(ms + EPS) * gain
        outs.append(_rope_pairs(xh, cos_t, sin_t, HEAD_DIM // 2) * scale)
    return jnp.concatenate(outs, axis=-1)


def _ep_kv(acc, gain, cos_t, sin_t):
    k = acc[:, :HEAD_DIM]
    ms = jnp.mean(k * k, axis=-1, keepdims=True)
    k = _rope_pairs(k * lax.rsqrt(ms + EPS) * gain, cos_t, sin_t, HEAD_DIM // 2)
    return jnp.concatenate([k, acc[:, HEAD_DIM:]], axis=-1)


def _ep_qi(acc, cos_t, sin_t):
    outs = []
    for c in range(acc.shape[1] // LANES):
        outs.append(_rope_pairs(acc[:, c * LANES:(c + 1) * LANES], cos_t, sin_t, IDX_DIM // 2))
    return jnp.concatenate(outs, axis=-1)


def _ep_small(acc, gain, cos_t, sin_t, *, idx_scale):
    lane = lax.broadcasted_iota(jnp.int32, acc.shape, 1)
    is_k = lane < IDX_DIM
    ms = jnp.sum(jnp.where(is_k, acc * acc, 0.0), axis=-1, keepdims=True) * (1.0 / IDX_DIM)
    k = _rope_pairs(acc * lax.rsqrt(ms + EPS) * gain, cos_t, sin_t, IDX_DIM // 2)
    return jnp.where(is_k, k, acc * idx_scale)


def _gmlp_kernel(u_ref, v_ref, gain_ref, ws_ref, bs_ref, o_ref):
    groups = ws_ref.shape[0]
    v = v_ref[...].astype(f32)
    ms = jnp.mean(v * v, axis=-1, keepdims=True)
    vn = (v * lax.rsqrt(ms + EPS) * gain_ref[...]).astype(bf16)
    row_chunk = lax.broadcasted_iota(jnp.int32, (MIX_BLOCK, MIX_BLOCK), 0) // CHUNK
    col_chunk = lax.broadcasted_iota(jnp.int32, (MIX_BLOCK, MIX_BLOCK), 1) // CHUNK
    causal = col_chunk <= row_chunk
    bs = bs_ref[...]
    for g in range(groups):
        sl = slice(g * GROUP_DIM, (g + 1) * GROUP_DIM)
        w = jnp.where(causal, ws_ref[g], 0.0).astype(bf16)
        mixed = jnp.dot(w, vn[:, sl], preferred_element_type=f32) + bs[:, g:g + 1]
        o_ref[:, sl] = (u_ref[:, sl].astype(f32) * mixed).astype(o_ref.dtype)


def gmlp_gate(uv, gain, ws, bs_t):
    n = uv.shape[0]
    gw = uv.shape[1] // 2
    groups = gw // GROUP_DIM
    return pl.pallas_call(
        _gmlp_kernel,
        grid=(n // MIX_BLOCK,),
        in_specs=[pl.BlockSpec((MIX_BLOCK, gw), lambda i: (i, 0)),
                  pl.BlockSpec((MIX_BLOCK, gw), lambda i: (i, 1)),
                  pl.BlockSpec((1, gw), lambda i: (0, 0)),
                  pl.BlockSpec((groups, MIX_BLOCK, MIX_BLOCK), lambda i: (0, 0, 0)),
                  pl.BlockSpec((MIX_BLOCK, groups), lambda i: (0, 0))],
        out_specs=pl.BlockSpec((MIX_BLOCK, gw), lambda i: (i, 0)),
        out_shape=jax.ShapeDtypeStruct((n, gw), bf16),
        compiler_params=_params(("parallel",)),
        name="gmlp_gate",
    )(uv, uv, gain.reshape(1, gw), ws, bs_t)


def _sortable(x):
    bits = lax.bitcast_convert_type(x, jnp.int32)
    return bits ^ ((bits >> 31) & jnp.int32(0x7FFFFFFF))


def _dsa_kernel(q_ref, kv_ref, qi_ref, smallk_ref, smallq_ref, o_ref,
                qit_ref, q2t_ref, vt_ref, keys_ref, m_ref, l_ref, acc_ref,
                *, n_qblk, top):
    qb = pl.program_id(0) % n_qblk
    n_kt = qb + 1
    n_grp = (n_kt + SELECT_UNROLL - 1) // SELECT_UNROLL
    heads = q_ref.shape[1] // HEAD_DIM
    idx_heads = qi_ref.shape[1] // IDX_DIM

    @pl.when(qb == 0)
    def _():
        def vt_tile(kt, carry):
            base = pl.multiple_of(kt * KEY_TILE, KEY_TILE)
            v_t = kv_ref[pl.ds(base, KEY_TILE), :][:, HEAD_DIM:]
            vt_ref[kt] = jnp.transpose(v_t.astype(f32)).astype(bf16)
            return carry

        lax.fori_loop(0, n_qblk * Q_BLOCK // KEY_TILE, vt_tile, 0)

    for c in range(qi_ref.shape[1] // LANES):
        t = jnp.transpose(qi_ref[:, c * LANES:(c + 1) * LANES].astype(f32)).astype(bf16)
        qit_ref[:, (2 * c) * Q_BLOCK:(2 * c + 1) * Q_BLOCK] = t[:IDX_DIM]
        qit_ref[:, (2 * c + 1) * Q_BLOCK:(2 * c + 2) * Q_BLOCK] = t[IDX_DIM:]
    for h in range(heads):
        hs = slice(h * HEAD_DIM, (h + 1) * HEAD_DIM)
        q2t_ref[:, hs] = jnp.transpose(q_ref[:, hs].astype(f32)).astype(bf16)
    w_t = jnp.transpose(smallq_ref[...])
    q_chunk = (qb * Q_BLOCK + lax.broadcasted_iota(jnp.int32, (KEY_TILE, Q_BLOCK), 1)) // CHUNK
    key_off = lax.broadcasted_iota(jnp.int32, (KEY_TILE, Q_BLOCK), 0)

    def score_tile(kt, carry):
        base = pl.multiple_of(kt * KEY_TILE, KEY_TILE)
        ki = smallk_ref[pl.ds(base, KEY_TILE), :][:, :IDX_DIM].astype(bf16)
        rel = jnp.dot(ki, qit_ref[...], preferred_element_type=f32)
        s = jnp.zeros((KEY_TILE, Q_BLOCK), f32)
        for h in range(idx_heads):
            r = jnp.maximum(rel[:, h * Q_BLOCK:(h + 1) * Q_BLOCK], 0.0)
            s = s + w_t[IDX_DIM + h:IDX_DIM + h + 1, :] * r
        adm = (base + key_off) // CHUNK <= q_chunk
        keys_ref[kt] = jnp.where(adm, _sortable(s), INT_MIN)
        return carry

    lax.fori_loop(0, n_kt, score_tile, 0)
    for j in range(SELECT_UNROLL - 1):
        @pl.when(n_kt + j < n_grp * SELECT_UNROLL)
        def _():
            keys_ref[n_kt + j] = jnp.full((KEY_TILE, Q_BLOCK), INT_MIN, jnp.int32)

    def count_keys(pred):
        def count_group(g, cnt):
            for u in range(SELECT_UNROLL):
                kt = g * SELECT_UNROLL + u
                hit = pred(keys_ref[kt], kt * KEY_TILE + key_off).astype(jnp.int32)
                cnt = cnt + jnp.sum(hit.reshape(KEY_TILE // 8, 8, Q_BLOCK), axis=0)
            return cnt

        cnt = lax.fori_loop(0, n_grp, count_group, jnp.zeros((8, Q_BLOCK), jnp.int32))
        return jnp.sum(cnt, axis=0, keepdims=True)

    def bit_step(i, carry):
        prefix, n_ge = carry
        cand = prefix + jnp.left_shift(jnp.int32(1), 31 - i)
        total = count_keys(lambda kk, idx: kk >= cand)
        take = total >= top
        return jnp.where(take, cand, prefix), jnp.where(take, total, n_ge)

    thr, n_ge = lax.fori_loop(
        0, 32, bit_step,
        (jnp.full((1, Q_BLOCK), INT_MIN, jnp.int32), jnp.zeros((1, Q_BLOCK), jnp.int32)))

    def tie_search(_):
        def idx_step(i, bound):
            cand = bound + jnp.left_shift(jnp.int32(1), idx_bits - 1 - i)
            total = count_keys(lambda kk, idx: (kk > thr) | ((kk == thr) & (idx < cand)))
            return jnp.where(total < top, cand, bound)

        return lax.fori_loop(0, idx_bits, idx_step, jnp.zeros((1, Q_BLOCK), jnp.int32))

    idx_bits = int(keys_ref.shape[0] * KEY_TILE).bit_length()
    no_ties = lambda _: jnp.full((1, Q_BLOCK), keys_ref.shape[0] * KEY_TILE, jnp.int32)
    last_idx = lax.cond(jnp.max(n_ge) > top, tie_search, no_ties, 0)

    m_ref[...] = jnp.full(m_ref.shape, NEG_BIG, f32)
    l_ref[...] = jnp.zeros(l_ref.shape, f32)
    acc_ref[...] = jnp.zeros(acc_ref.shape, f32)

    def attn_tile(kt, carry):
        base = pl.multiple_of(kt * KEY_TILE, KEY_TILE)
        k_t = kv_ref[pl.ds(base, KEY_TILE), :][:, :HEAD_DIM]
        logits = jnp.dot(k_t, q2t_ref[...], preferred_element_type=f32)
        kk = keys_ref[kt]
        idx = base + key_off
        sel = ((kk > thr) | ((kk == thr) & (idx <= last_idx))) & (kk != INT_MIN)
        probs, alphas = [], []
        for h in range(heads):
            hs = slice(h * Q_BLOCK, (h + 1) * Q_BLOCK)
            lg = jnp.where(sel, logits[:, hs], NEG_BIG)
            m_old = m_ref[:, hs]
            m_new = jnp.maximum(m_old, jnp.max(lg, axis=0, keepdims=True))
            alpha = jnp.exp(m_old - m_new)
            p = jnp.exp(lg - m_new)
            l_ref[:, hs] = alpha * l_ref[:, hs] + jnp.sum(p, axis=0, keepdims=True)
            m_ref[:, hs] = m_new
            probs.append(p.astype(bf16))
            alphas.append(alpha)
        pv = jnp.dot(vt_ref[kt], jnp.concatenate(probs, axis=1),
                     preferred_element_type=f32)
        acc_ref[...] = acc_ref[...] * jnp.concatenate(alphas, axis=1) + pv
        return carry

    lax.fori_loop(0, n_kt, attn_tile, 0)

    out = acc_ref[...] / l_ref[...]
    for h in range(heads):
        hs = slice(h * HEAD_DIM, (h + 1) * HEAD_DIM)
        o_ref[:, hs] = jnp.transpose(out[:, hs]).astype(o_ref.dtype)


def dsa_attention(q, kv, qi, small, batch, seq):
    n, aw = q.shape
    n_qblk = seq // Q_BLOCK
    n_kt = seq // KEY_TILE
    n_kt_pad = -(-n_kt // SELECT_UNROLL) * SELECT_UNROLL
    heads = aw // HEAD_DIM
    iw = qi.shape[1]
    top = min(TOPK_MAX, seq // 4)
    return pl.pallas_call(
        functools.partial(_dsa_kernel, n_qblk=n_qblk, top=top),
        grid=(batch * n_qblk,),
        in_specs=[pl.BlockSpec((Q_BLOCK, aw), lambda g: (g, 0)),
                  pl.BlockSpec((seq, 2 * HEAD_DIM), lambda g: (g // n_qblk, 0)),
                  pl.BlockSpec((Q_BLOCK, iw), lambda g: (g, 0)),
                  pl.BlockSpec((seq, LANES), lambda g: (g // n_qblk, 0)),
                  pl.BlockSpec((Q_BLOCK, LANES), lambda g: (g, 0))],
        out_specs=pl.BlockSpec((Q_BLOCK, aw), lambda g: (g, 0)),
        out_shape=jax.ShapeDtypeStruct((n, aw), bf16),
        scratch_shapes=[pltpu.VMEM((IDX_DIM, (iw // IDX_DIM) * Q_BLOCK), bf16),
                        pltpu.VMEM((HEAD_DIM, heads * Q_BLOCK), bf16),
                        pltpu.VMEM((n_kt, HEAD_DIM, KEY_TILE), bf16),
                        pltpu.VMEM((n_kt_pad, KEY_TILE, Q_BLOCK), jnp.int32),
                        pltpu.VMEM((1, heads * Q_BLOCK), f32),
                        pltpu.VMEM((1, heads * Q_BLOCK), f32),
                        pltpu.VMEM((HEAD_DIM, heads * Q_BLOCK), f32)],
        compiler_params=_params(("arbitrary",)),
        name="dsa_attention",
    )(q, kv, qi, small, small)


def _merge_kernel(ya_ref, yb_ref, wa_ref, wb_ref, ga_ref, gb_ref, o_ref):
    a = jnp.dot(ya_ref[...], wa_ref[...], preferred_element_type=f32)
    b = jnp.dot(yb_ref[...], wb_ref[...], preferred_element_type=f32)
    o_ref[...] = (ga_ref[...].astype(f32) * a + gb_ref[...].astype(f32) * b).astype(o_ref.dtype)


def gated_merge(ya, yb, wa, wb, gates, tm=1024, tn=512):
    n, ka = ya.shape
    kb = yb.shape[1]
    d = wa.shape[1]
    tm, tn = _tile(n, tm), _tile(d, tn)
    nj = d // tn
    assert d % tn == 0
    return pl.pallas_call(
        _merge_kernel,
        grid=(pl.cdiv(n, tm), nj),
        in_specs=[pl.BlockSpec((tm, ka), lambda i, j: (i, 0)),
                  pl.BlockSpec((tm, kb), lambda i, j: (i, 0)),
                  pl.BlockSpec((ka, tn), lambda i, j: (0, j)),
                  pl.BlockSpec((kb, tn), lambda i, j: (0, j)),
                  pl.BlockSpec((tm, tn), lambda i, j: (i, j)),
                  pl.BlockSpec((tm, tn), lambda i, j: (i, j + nj))],
        out_specs=pl.BlockSpec((tm, tn), lambda i, j: (i, j)),
        out_shape=jax.ShapeDtypeStruct((n, d), bf16),
        compiler_params=_params(("parallel", "arbitrary")),
        name="gated_merge",
    )(ya, yb, wa, wb, gates, gates)


def _rope_tables(seq, dim, lanes):
    inv = ROPE_THETA ** (-jnp.arange(0, dim, 2, dtype=f32) / dim)
    ang = jnp.arange(seq, dtype=f32)[:, None] * inv[None, :]
    cos, sin = jnp.cos(ang), jnp.sin(ang)
    reps = lanes // dim
    cos_t = jnp.tile(jnp.concatenate([cos, cos], axis=-1), (1, reps))
    sin_t = jnp.tile(jnp.concatenate([-sin, sin], axis=-1), (1, reps))
    return cos_t, sin_t


def _ffn(h, gain, w1, w3, w2, nk_down):
    xn = rmsnorm_bf16(h, gain)
    g = swiglu_up(xn, w1.astype(bf16), w3.astype(bf16))
    return matmul(g, w2.astype(bf16), _ep_half_residual, f32, extras=[(h, "tile")],
                  nk=nk_down, name="ffn_down")


def kernel(x, ffn1_norm, ffn1_w1, ffn1_w3, ffn1_w2, mix_norm, w_in, gate_bias,
           gmlp_v_norm, gmlp_ws, gmlp_bs, q_norm, k_norm, idx_k_norm,
           w_br_a, w_br_b, w_out, ffn2_norm, ffn2_w1, ffn2_w3, ffn2_w2):
    batch, seq, d = x.shape
    depth = ffn1_norm.shape[0]
    n = batch * seq
    gw = gmlp_v_norm.shape[1]
    aw = w_br_b.shape[1]
    idx_heads = d // 128
    iw = idx_heads * IDX_DIM
    d_ff = ffn1_w1.shape[2]
    nk_down = 2 if (d_ff % (2 * LANES) == 0 and d_ff > 4096) else 1
    c_uv, c_q, c_kv, c_qi = 2 * gw, aw, 2 * HEAD_DIM, iw
    o_q = c_uv
    o_kv = o_q + c_q
    o_qi = o_kv + c_kv
    o_small = o_qi + c_qi
    o_gate = o_small + IDX_DIM + idx_heads
    assert o_gate + 2 * d == w_in.shape[2]
    assert IDX_DIM + idx_heads <= LANES
    idx_scale = (idx_heads ** -0.5) * (IDX_DIM ** -0.5)

    cos_a, sin_a = _rope_tables(seq, HEAD_DIM, LANES)
    cos_i, sin_i = _rope_tables(seq, IDX_DIM, LANES)
    tm = _tile(seq, 1024)
    assert seq % tm == 0 and seq % Q_BLOCK == 0

    h = x.reshape(n, d)
    for l in range(depth):
        h = _ffn(h, ffn1_norm[l], ffn1_w1[l], ffn1_w3[l], ffn1_w2[l], nk_down)

        nrm = rmsnorm_bf16(h, mix_norm[l])
        wl = w_in[l]
        w_uv = wl[:, :c_uv].astype(bf16)
        w_q = wl[:, o_q:o_q + c_q].astype(bf16)
        w_kv = wl[:, o_kv:o_kv + c_kv].astype(bf16)
        w_qi = wl[:, o_qi:o_qi + c_qi].astype(bf16)
        w_small = jnp.pad(wl[:, o_small:o_gate], ((0, 0), (0, LANES - (o_gate - o_small)))).astype(bf16)
        w_gate = wl[:, o_gate:].astype(bf16)

        uv = matmul(nrm, w_uv, _ep_gelu, bf16, tm=tm, name="proj_uv")
        q = matmul(nrm, w_q, functools.partial(_ep_q, scale=HEAD_DIM ** -0.5), bf16,
                   extras=[(q_norm[l].reshape(1, HEAD_DIM), "const"),
                           (cos_a, "pos"), (sin_a, "pos")], tm=tm, name="proj_q")
        kv = matmul(nrm, w_kv, _ep_kv, bf16,
                    extras=[(k_norm[l].reshape(1, HEAD_DIM), "const"),
                            (cos_a, "pos"), (sin_a, "pos")], tm=tm, name="proj_kv")
        qi = matmul(nrm, w_qi, _ep_qi, bf16,
                    extras=[(cos_i, "pos"), (sin_i, "pos")], tm=tm, name="proj_qi")
        k_gain = jnp.tile(idx_k_norm[l].reshape(1, IDX_DIM), (1, LANES // IDX_DIM))
        small = matmul(nrm, w_small, functools.partial(_ep_small, idx_scale=idx_scale), f32,
                       extras=[(k_gain, "const"), (cos_i, "pos"), (sin_i, "pos")],
                       tm=tm, name="proj_small")
        gates = matmul(nrm, w_gate, _ep_gate, bf16,
                       extras=[(gate_bias[l].reshape(1, 2 * d), "row")], tm=tm, name="proj_gate")

        ya = gmlp_gate(uv, gmlp_v_norm[l], gmlp_ws[l], gmlp_bs[l].T)
        yb = dsa_attention(q, kv, qi, small, batch, seq)
        m = gated_merge(ya, yb, w_br_a[l].astype(bf16), w_br_b[l].astype(bf16), gates, tm=tm)
        h = matmul(m, w_out[l].astype(bf16), _ep_residual, f32, extras=[(h, "tile")],
                   tm=tm, name="out_proj")

        h = _ffn(h, ffn2_norm[l], ffn2_w1[l], ffn2_w3[l], ffn2_w2[l], nk_down)
    return h.reshape(batch, seq, d)
```

```python
import functools
import math

import jax
import jax.numpy as jnp
import numpy as np
from jax import lax
from jax.experimental import pallas as pl
from jax.experimental.pallas import tpu as pltpu

CHUNK = 64
GROUP_DIM = 128
MIX_BLOCK = 128
HEAD_DIM = 128
IDX_DIM = 64
Q_BLOCK = 128
KEY_TILE = 128
SCORE_GROUP = 4
ATTN_GROUP = 4
TOPK_MAX = 256
ROPE_THETA = 10000.0
EPS = 1e-6
LANES = 128
NEG_BIG = -1e30
INT_MIN = -(2 ** 31)
VMEM_LIMIT_BYTES = 60000 * 1024
WIDE_TN = 1024

f32 = jnp.float32
bf16 = jnp.bfloat16


def _params(sem):
    return pltpu.CompilerParams(dimension_semantics=sem,
                                vmem_limit_bytes=VMEM_LIMIT_BYTES)


def _tile(dim, want):
    return dim if dim <= want else want


def _gelu(x):
    return 0.5 * x * (1.0 + lax.erf(x * (2.0 ** -0.5)))


def _rope_pairs(x, cos_t, sin_t, half):
    width = x.shape[-1]
    lane = lax.broadcasted_iota(jnp.int32, x.shape, x.ndim - 1)
    fwd = pltpu.roll(x, width - half, axis=x.ndim - 1)
    bwd = pltpu.roll(x, half, axis=x.ndim - 1)
    partner = jnp.where((lane % (2 * half)) < half, fwd, bwd)
    return x * cos_t + partner * sin_t


def _rmsnorm_kernel(x_ref, g_ref, o_ref):
    x = x_ref[...]
    ms = jnp.mean(x * x, axis=-1, keepdims=True)
    o_ref[...] = (x * lax.rsqrt(ms + EPS) * g_ref[...]).astype(o_ref.dtype)


def rmsnorm_bf16(x, gain, tm=256):
    n, d = x.shape
    tm = _tile(n, tm)
    return pl.pallas_call(
        _rmsnorm_kernel,
        grid=(pl.cdiv(n, tm),),
        in_specs=[pl.BlockSpec((tm, d), lambda i: (i, 0)),
                  pl.BlockSpec((1, d), lambda i: (0, 0))],
        out_specs=pl.BlockSpec((tm, d), lambda i: (i, 0)),
        out_shape=jax.ShapeDtypeStruct((n, d), bf16),
        compiler_params=_params(("parallel",)),
        name="rmsnorm",
    )(x, gain.reshape(1, d))


def _swiglu_up_kernel(x_ref, w1_ref, w3_ref, o_ref):
    x = x_ref[...]
    a = jnp.dot(x, w1_ref[...], preferred_element_type=f32)
    b = jnp.dot(x, w3_ref[...], preferred_element_type=f32)
    o_ref[...] = (a * jax.nn.sigmoid(a) * b).astype(o_ref.dtype)


def swiglu_up(xn, w1, w3, tm=1024, tn=512):
    n, d = xn.shape
    f = w1.shape[1]
    tm, tn = _tile(n, tm), _tile(f, tn)
    return pl.pallas_call(
        _swiglu_up_kernel,
        grid=(pl.cdiv(n, tm), pl.cdiv(f, tn)),
        in_specs=[pl.BlockSpec((tm, d), lambda i, j: (i, 0)),
                  pl.BlockSpec((d, tn), lambda i, j: (0, j)),
                  pl.BlockSpec((d, tn), lambda i, j: (0, j))],
        out_specs=pl.BlockSpec((tm, tn), lambda i, j: (i, j)),
        out_shape=jax.ShapeDtypeStruct((n, f), bf16),
        compiler_params=_params(("parallel", "arbitrary")),
        name="swiglu_up",
    )(xn, w1, w3)


def _mm_kernel(*refs, nk, n_extra, epilogue):
    a_ref, w_ref = refs[0], refs[1]
    extra = refs[2:2 + n_extra]
    o_ref = refs[2 + n_extra]
    part = jnp.dot(a_ref[...], w_ref[...], preferred_element_type=f32)
    if nk == 1:
        o_ref[...] = epilogue(part, *[e[...] for e in extra]).astype(o_ref.dtype)
        return
    acc_ref = refs[3 + n_extra]
    k = pl.program_id(2)

    @pl.when(k == 0)
    def _():
        acc_ref[...] = part

    @pl.when(k > 0)
    def _():
        acc_ref[...] += part

    @pl.when(k == nk - 1)
    def _():
        o_ref[...] = epilogue(acc_ref[...], *[e[...] for e in extra]).astype(o_ref.dtype)


def matmul(a, w, epilogue, out_dtype, extras=(), tm=1024, tn=512, nk=1, name="mm"):
    m, kdim = a.shape
    n = w.shape[1]
    tm, tn = _tile(m, tm), _tile(n, tn)
    assert kdim % nk == 0
    tk = kdim // nk
    assert nk == 1 or tk % LANES == 0
    in_specs = [pl.BlockSpec((tm, tk), lambda i, j, k: (i, k)),
                pl.BlockSpec((tk, tn), lambda i, j, k: (k, j))]
    arrays = []
    for arr, kind in extras:
        arrays.append(arr)
        if kind == "tile":
            in_specs.append(pl.BlockSpec((tm, tn), lambda i, j, k: (i, j)))
        elif kind == "row":
            in_specs.append(pl.BlockSpec((1, tn), lambda i, j, k: (0, j)))
        elif kind == "const":
            in_specs.append(pl.BlockSpec(arr.shape, lambda i, j, k: (0, 0)))
        elif kind == "pos":
            assert arr.shape[0] % tm == 0
            nper = arr.shape[0] // tm
            in_specs.append(pl.BlockSpec((tm, arr.shape[1]),
                                         lambda i, j, k, nper=nper: (i % nper, 0)))
        else:
            raise ValueError(kind)
    scratch = [pltpu.VMEM((tm, tn), f32)] if nk > 1 else []
    return pl.pallas_call(
        functools.partial(_mm_kernel, nk=nk, n_extra=len(arrays), epilogue=epilogue),
        grid=(pl.cdiv(m, tm), pl.cdiv(n, tn), nk),
        in_specs=in_specs,
        out_specs=pl.BlockSpec((tm, tn), lambda i, j, k: (i, j)),
        out_shape=jax.ShapeDtypeStruct((m, n), out_dtype),
        scratch_shapes=scratch,
        compiler_params=_params(("parallel", "arbitrary", "arbitrary")),
        name=name,
    )(a, w, *arrays)


def _ep_half_residual(acc, res):
    return res + 0.5 * acc


def _ep_residual(acc, res):
    return res + acc


def _ep_gelu(acc):
    return _gelu(acc)


def _ep_gate(acc, bias):
    return jax.nn.sigmoid(acc + bias)


def _ep_identity(acc):
    return acc


def _ep_kv(acc, gain, cos_t, sin_t):
    k = acc[:, :HEAD_DIM]
    ms = jnp.mean(k * k, axis=-1, keepdims=True)
    k = _rope_pairs(k * lax.rsqrt(ms + EPS) * gain, cos_t, sin_t, HEAD_DIM // 2)
    return jnp.concatenate([k, acc[:, HEAD_DIM:]], axis=-1)


def _ep_small(acc, gain, cos_t, sin_t, *, idx_scale):
    lane = lax.broadcasted_iota(jnp.int32, acc.shape, 1)
    is_k = lane < IDX_DIM
    ms = jnp.sum(jnp.where(is_k, acc * acc, 0.0), axis=-1, keepdims=True) * (1.0 / IDX_DIM)
    k = _rope_pairs(acc * lax.rsqrt(ms + EPS) * gain, cos_t, sin_t, IDX_DIM // 2)
    return jnp.where(is_k, k, acc * idx_scale)


def _gmlp_kernel(u_ref, v_ref, gain_ref, ws_ref, bs_ref, o_ref):
    groups = ws_ref.shape[0]
    v = v_ref[...].astype(f32)
    ms = jnp.mean(v * v, axis=-1, keepdims=True)
    vn = (v * lax.rsqrt(ms + EPS) * gain_ref[...]).astype(bf16)
    row_chunk = lax.broadcasted_iota(jnp.int32, (MIX_BLOCK, MIX_BLOCK), 0) // CHUNK
    col_chunk = lax.broadcasted_iota(jnp.int32, (MIX_BLOCK, MIX_BLOCK), 1) // CHUNK
    causal = col_chunk <= row_chunk
    bs = bs_ref[...]
    for g in range(groups):
        sl = slice(g * GROUP_DIM, (g + 1) * GROUP_DIM)
        w = jnp.where(causal, ws_ref[g], 0.0).astype(bf16)
        mixed = jnp.dot(w, vn[:, sl], preferred_element_type=f32) + bs[:, g:g + 1]
        o_ref[:, sl] = (u_ref[:, sl].astype(f32) * mixed).astype(o_ref.dtype)


def gmlp_gate(uv, gain, ws, bs_t):
    n = uv.shape[0]
    gw = uv.shape[1] // 2
    groups = gw // GROUP_DIM
    return pl.pallas_call(
        _gmlp_kernel,
        grid=(n // MIX_BLOCK,),
        in_specs=[pl.BlockSpec((MIX_BLOCK, gw), lambda i: (i, 0)),
                  pl.BlockSpec((MIX_BLOCK, gw), lambda i: (i, 1)),
                  pl.BlockSpec((1, gw), lambda i: (0, 0)),
                  pl.BlockSpec((groups, MIX_BLOCK, MIX_BLOCK), lambda i: (0, 0, 0)),
                  pl.BlockSpec((MIX_BLOCK, groups), lambda i: (0, 0))],
        out_specs=pl.BlockSpec((MIX_BLOCK, gw), lambda i: (i, 0)),
        out_shape=jax.ShapeDtypeStruct((n, gw), bf16),
        compiler_params=_params(("parallel",)),
        name="gmlp_gate",
    )(uv, uv, gain.reshape(1, gw), ws, bs_t)


def _sortable(x):
    bits = lax.bitcast_convert_type(x, jnp.int32)
    return bits ^ ((bits >> 31) & jnp.int32(0x7FFFFFFF))


def _dsa_kernel(q_ref, kv_ref, qi_ref, smallk_ref, smallq_ref, qgain_ref,
                cos_a_ref, sin_a_ref, cos_i_ref, sin_i_ref, o_ref,
                qit_ref, q2t_ref, vt_ref, keys_ref, m_ref, l_ref, acc_ref,
                *, n_qblk, top):
    qb = pl.program_id(0) % n_qblk
    n_kt = qb + 1
    n_grp = (n_kt + SCORE_GROUP - 1) // SCORE_GROUP
    n_grp_attn = (n_kt + ATTN_GROUP - 1) // ATTN_GROUP
    heads = q_ref.shape[1] // HEAD_DIM
    idx_heads = qi_ref.shape[1] // IDX_DIM

    @pl.when(qb == 0)
    def _():
        def vt_tile(kt, carry):
            base = pl.multiple_of(kt * KEY_TILE, KEY_TILE)
            v_t = kv_ref[pl.ds(base, KEY_TILE), :][:, HEAD_DIM:]
            vt_ref[kt] = jnp.transpose(v_t.astype(f32)).astype(bf16)
            return carry

        lax.fori_loop(0, n_qblk * Q_BLOCK // KEY_TILE, vt_tile, 0)

    def rope_rows(x, cos_t, sin_t):
        half = x.shape[0] // 2
        x1, x2 = x[:half], x[half:]
        return jnp.concatenate([x1 * cos_t - x2 * sin_t, x1 * sin_t + x2 * cos_t], axis=0)

    cos_i, sin_i = cos_i_ref[...], sin_i_ref[...]
    for c in range(qi_ref.shape[1] // LANES):
        t = jnp.transpose(qi_ref[:, c * LANES:(c + 1) * LANES])
        for j in range(LANES // IDX_DIM):
            h = c * (LANES // IDX_DIM) + j
            qit_ref[:, h * Q_BLOCK:(h + 1) * Q_BLOCK] = rope_rows(
                t[j * IDX_DIM:(j + 1) * IDX_DIM], cos_i, sin_i).astype(bf16)
    cos_a, sin_a, qgain = cos_a_ref[...], sin_a_ref[...], qgain_ref[...]
    for h in range(heads):
        hs = slice(h * HEAD_DIM, (h + 1) * HEAD_DIM)
        t = jnp.transpose(q_ref[:, hs])
        ms = jnp.mean(t * t, axis=0, keepdims=True)
        t = rope_rows(t * lax.rsqrt(ms + EPS) * qgain, cos_a, sin_a)
        q2t_ref[:, hs] = (t * (HEAD_DIM ** -0.5)).astype(bf16)
    w_t = jnp.transpose(smallq_ref[...])
    q_chunk = (qb * Q_BLOCK + lax.broadcasted_iota(jnp.int32, (KEY_TILE, Q_BLOCK), 1)) // CHUNK
    key_off = lax.broadcasted_iota(jnp.int32, (KEY_TILE, Q_BLOCK), 0)

    def score_group(g, carry):
        rows = SCORE_GROUP * KEY_TILE
        base = pl.multiple_of(g * rows, rows)
        ki = smallk_ref[pl.ds(base, rows), :][:, :IDX_DIM].astype(bf16)
        rel = jnp.dot(ki, qit_ref[...], preferred_element_type=f32)
        for u in range(SCORE_GROUP):
            us = slice(u * KEY_TILE, (u + 1) * KEY_TILE)
            s = jnp.zeros((KEY_TILE, Q_BLOCK), f32)
            for h in range(idx_heads):
                r = jnp.maximum(rel[us, h * Q_BLOCK:(h + 1) * Q_BLOCK], 0.0)
                s = s + w_t[IDX_DIM + h:IDX_DIM + h + 1, :] * r
            adm = (base + u * KEY_TILE + key_off) // CHUNK <= q_chunk
            keys_ref[g * SCORE_GROUP + u] = jnp.where(adm, _sortable(s), INT_MIN)
        return carry

    lax.fori_loop(0, n_grp, score_group, 0)

    def count_keys(pred):
        def count_group(g, cnt):
            for u in range(SCORE_GROUP):
                kt = g * SCORE_GROUP + u
                hit = pred(keys_ref[kt], kt * KEY_TILE + key_off).astype(jnp.int32)
                cnt = cnt + jnp.sum(hit.reshape(KEY_TILE // 8, 8, Q_BLOCK), axis=0)
            return cnt

        cnt = lax.fori_loop(0, n_grp, count_group, jnp.zeros((8, Q_BLOCK), jnp.int32))
        return jnp.sum(cnt, axis=0, keepdims=True)

    def bit_step(i, carry):
        prefix, n_ge = carry
        cand = prefix + jnp.left_shift(jnp.int32(1), 31 - i)
        total = count_keys(lambda kk, idx: kk >= cand)
        take = total >= top
        return jnp.where(take, cand, prefix), jnp.where(take, total, n_ge)

    thr, n_ge = lax.fori_loop(
        0, 32, bit_step,
        (jnp.full((1, Q_BLOCK), INT_MIN, jnp.int32), jnp.zeros((1, Q_BLOCK), jnp.int32)))

    def tie_search(_):
        def idx_step(i, bound):
            cand = bound + jnp.left_shift(jnp.int32(1), idx_bits - 1 - i)
            total = count_keys(lambda kk, idx: (kk > thr) | ((kk == thr) & (idx < cand)))
            return jnp.where(total < top, cand, bound)

        return lax.fori_loop(0, idx_bits, idx_step, jnp.zeros((1, Q_BLOCK), jnp.int32))

    idx_bits = int(keys_ref.shape[0] * KEY_TILE).bit_length()
    no_ties = lambda _: jnp.full((1, Q_BLOCK), keys_ref.shape[0] * KEY_TILE, jnp.int32)
    last_idx = lax.cond(jnp.max(n_ge) > top, tie_search, no_ties, 0)

    m_ref[...] = jnp.full(m_ref.shape, NEG_BIG, f32)
    l_ref[...] = jnp.zeros(l_ref.shape, f32)
    acc_ref[...] = jnp.zeros(acc_ref.shape, f32)

    def attn_group(g, carry):
        rows = ATTN_GROUP * KEY_TILE
        base = pl.multiple_of(g * rows, rows)
        k_t = kv_ref[pl.ds(base, rows), :][:, :HEAD_DIM]
        logits = jnp.dot(k_t, q2t_ref[...], preferred_element_type=f32)
        sel = []
        for u in range(ATTN_GROUP):
            kk = keys_ref[g * ATTN_GROUP + u]
            idx = base + u * KEY_TILE + key_off
            sel.append(((kk > thr) | ((kk == thr) & (idx <= last_idx))) & (kk != INT_MIN))
        sel = jnp.concatenate(sel, axis=0)
        probs, alphas = [], []
        for h in range(heads):
            hs = slice(h * Q_BLOCK, (h + 1) * Q_BLOCK)
            lg = jnp.where(sel, logits[:, hs], NEG_BIG)
            m_old = m_ref[:, hs]
            m_new = jnp.maximum(m_old, jnp.max(lg, axis=0, keepdims=True))
            alpha = jnp.exp(m_old - m_new)
            p = jnp.exp(lg - m_new)
            l_ref[:, hs] = alpha * l_ref[:, hs] + jnp.sum(p, axis=0, keepdims=True)
            m_ref[:, hs] = m_new
            probs.append(p.astype(bf16))
            alphas.append(alpha)
        v_t = jnp.concatenate([vt_ref[g * ATTN_GROUP + u] for u in range(ATTN_GROUP)], axis=1)
        pv = jnp.dot(v_t, jnp.concatenate(probs, axis=1),
                     preferred_element_type=f32)
        acc_ref[...] = acc_ref[...] * jnp.concatenate(alphas, axis=1) + pv
        return carry

    lax.fori_loop(0, n_grp_attn, attn_group, 0)

    out = acc_ref[...] / l_ref[...]
    for h in range(heads):
        hs = slice(h * HEAD_DIM, (h + 1) * HEAD_DIM)
        o_ref[:, hs] = jnp.transpose(out[:, hs]).astype(o_ref.dtype)


def dsa_attention(q, kv, qi, small, q_gain, rope_a, rope_i, batch, seq):
    n, aw = q.shape
    n_qblk = seq // Q_BLOCK
    n_kt = seq // KEY_TILE
    assert n_kt % SCORE_GROUP == 0 and SCORE_GROUP % ATTN_GROUP == 0
    heads = aw // HEAD_DIM
    iw = qi.shape[1]
    top = min(TOPK_MAX, seq // 4)
    q_gain_t = jnp.broadcast_to(q_gain.reshape(HEAD_DIM, 1), (HEAD_DIM, Q_BLOCK))
    per_qblk = lambda rows: pl.BlockSpec((rows, Q_BLOCK), lambda g: (0, g % n_qblk))
    return pl.pallas_call(
        functools.partial(_dsa_kernel, n_qblk=n_qblk, top=top),
        grid=(batch * n_qblk,),
        in_specs=[pl.BlockSpec((Q_BLOCK, aw), lambda g: (g, 0)),
                  pl.BlockSpec((seq, 2 * HEAD_DIM), lambda g: (g // n_qblk, 0)),
                  pl.BlockSpec((Q_BLOCK, iw), lambda g: (g, 0)),
                  pl.BlockSpec((seq, LANES), lambda g: (g // n_qblk, 0)),
                  pl.BlockSpec((Q_BLOCK, LANES), lambda g: (g, 0)),
                  pl.BlockSpec((HEAD_DIM, Q_BLOCK), lambda g: (0, 0)),
                  per_qblk(HEAD_DIM // 2), per_qblk(HEAD_DIM // 2),
                  per_qblk(IDX_DIM // 2), per_qblk(IDX_DIM // 2)],
        out_specs=pl.BlockSpec((Q_BLOCK, aw), lambda g: (g, 0)),
        out_shape=jax.ShapeDtypeStruct((n, aw), bf16),
        scratch_shapes=[pltpu.VMEM((IDX_DIM, (iw // IDX_DIM) * Q_BLOCK), bf16),
                        pltpu.VMEM((HEAD_DIM, heads * Q_BLOCK), bf16),
                        pltpu.VMEM((n_kt, HEAD_DIM, KEY_TILE), bf16),
                        pltpu.VMEM((n_kt, KEY_TILE, Q_BLOCK), jnp.int32),
                        pltpu.VMEM((1, heads * Q_BLOCK), f32),
                        pltpu.VMEM((1, heads * Q_BLOCK), f32),
                        pltpu.VMEM((HEAD_DIM, heads * Q_BLOCK), f32)],
        compiler_params=_params(("arbitrary",)),
        name="dsa_attention",
    )(q, kv, qi, small, small, q_gain_t, *rope_a, *rope_i)


def _merge_kernel(ya_ref, yb_ref, wa_ref, wb_ref, ga_ref, gb_ref, o_ref):
    a = jnp.dot(ya_ref[...], wa_ref[...], preferred_element_type=f32)
    b = jnp.dot(yb_ref[...], wb_ref[...], preferred_element_type=f32)
    o_ref[...] = (ga_ref[...].astype(f32) * a + gb_ref[...].astype(f32) * b).astype(o_ref.dtype)


def gated_merge(ya, yb, wa, wb, gates, tm=1024, tn=512):
    n, ka = ya.shape
    kb = yb.shape[1]
    d = wa.shape[1]
    tm, tn = _tile(n, tm), _tile(d, tn)
    nj = d // tn
    assert d % tn == 0
    return pl.pallas_call(
        _merge_kernel,
        grid=(pl.cdiv(n, tm), nj),
        in_specs=[pl.BlockSpec((tm, ka), lambda i, j: (i, 0)),
                  pl.BlockSpec((tm, kb), lambda i, j: (i, 0)),
                  pl.BlockSpec((ka, tn), lambda i, j: (0, j)),
                  pl.BlockSpec((kb, tn), lambda i, j: (0, j)),
                  pl.BlockSpec((tm, tn), lambda i, j: (i, j)),
                  pl.BlockSpec((tm, tn), lambda i, j: (i, j + nj))],
        out_specs=pl.BlockSpec((tm, tn), lambda i, j: (i, j)),
        out_shape=jax.ShapeDtypeStruct((n, d), bf16),
        compiler_params=_params(("parallel", "arbitrary")),
        name="gated_merge",
    )(ya, yb, wa, wb, gates, gates)


def _rope_tables(seq, dim, lanes):
    inv = ROPE_THETA ** (-jnp.arange(0, dim, 2, dtype=f32) / dim)
    ang = jnp.arange(seq, dtype=f32)[:, None] * inv[None, :]
    cos, sin = jnp.cos(ang), jnp.sin(ang)
    reps = lanes // dim
    cos_t = jnp.tile(jnp.concatenate([cos, cos], axis=-1), (1, reps))
    sin_t = jnp.tile(jnp.concatenate([-sin, sin], axis=-1), (1, reps))
    return cos_t, sin_t


def _rope_tables_t(seq, dim):
    inv = ROPE_THETA ** (-jnp.arange(0, dim, 2, dtype=f32) / dim)
    ang = inv[:, None] * jnp.arange(seq, dtype=f32)[None, :]
    return jnp.cos(ang), jnp.sin(ang)


def _ffn(h, gain, w1, w3, w2, nk_down):
    xn = rmsnorm_bf16(h, gain)
    g = swiglu_up(xn, w1.astype(bf16), w3.astype(bf16))
    return matmul(g, w2.astype(bf16), _ep_half_residual, f32, extras=[(h, "tile")],
                  nk=nk_down, name="ffn_down")


def kernel(x, ffn1_norm, ffn1_w1, ffn1_w3, ffn1_w2, mix_norm, w_in, gate_bias,
           gmlp_v_norm, gmlp_ws, gmlp_bs, q_norm, k_norm, idx_k_norm,
           w_br_a, w_br_b, w_out, ffn2_norm, ffn2_w1, ffn2_w3, ffn2_w2):
    batch, seq, d = x.shape
    depth = ffn1_norm.shape[0]
    n = batch * seq
    gw = gmlp_v_norm.shape[1]
    aw = w_br_b.shape[1]
    idx_heads = d // 128
    iw = idx_heads * IDX_DIM
    d_ff = ffn1_w1.shape[2]
    nk_down = 2 if (d_ff % (2 * LANES) == 0 and d_ff > 4096) else 1
    c_uv, c_q, c_kv, c_qi = 2 * gw, aw, 2 * HEAD_DIM, iw
    o_q = c_uv
    o_kv = o_q + c_q
    o_qi = o_kv + c_kv
    o_small = o_qi + c_qi
    o_gate = o_small + IDX_DIM + idx_heads
    assert o_gate + 2 * d == w_in.shape[2]
    assert IDX_DIM + idx_heads <= LANES
    idx_scale = (idx_heads ** -0.5) * (IDX_DIM ** -0.5)

    cos_a, sin_a = _rope_tables(seq, HEAD_DIM, LANES)
    cos_i, sin_i = _rope_tables(seq, IDX_DIM, LANES)
    rope_a_t = _rope_tables_t(seq, HEAD_DIM)
    rope_i_t = _rope_tables_t(seq, IDX_DIM)
    tm = _tile(seq, 1024)
    assert seq % tm == 0 and seq % Q_BLOCK == 0

    h = x.reshape(n, d)
    for l in range(depth):
        h = _ffn(h, ffn1_norm[l], ffn1_w1[l], ffn1_w3[l], ffn1_w2[l], nk_down)

        nrm = rmsnorm_bf16(h, mix_norm[l])
        wl = w_in[l]
        w_uv = wl[:, :c_uv].astype(bf16)
        w_q = wl[:, o_q:o_q + c_q].astype(bf16)
        w_kv = wl[:, o_kv:o_kv + c_kv].astype(bf16)
        w_qi = wl[:, o_qi:o_qi + c_qi].astype(bf16)
        w_small = jnp.pad(wl[:, o_small:o_gate], ((0, 0), (0, LANES - (o_gate - o_small)))).astype(bf16)
        w_gate = wl[:, o_gate:].astype(bf16)

        uv = matmul(nrm, w_uv, _ep_gelu, bf16, tm=tm, tn=WIDE_TN, name="proj_uv")
        q = matmul(nrm, w_q, _ep_identity, f32, tm=tm, tn=WIDE_TN, name="proj_q")
        kv = matmul(nrm, w_kv, _ep_kv, bf16,
                    extras=[(k_norm[l].reshape(1, HEAD_DIM), "const"),
                            (cos_a, "pos"), (sin_a, "pos")], tm=tm, name="proj_kv")
        qi = matmul(nrm, w_qi, _ep_identity, f32, tm=tm, tn=WIDE_TN, name="proj_qi")
        k_gain = jnp.tile(idx_k_norm[l].reshape(1, IDX_DIM), (1, LANES // IDX_DIM))
        small = matmul(nrm, w_small, functools.partial(_ep_small, idx_scale=idx_scale), f32,
                       extras=[(k_gain, "const"), (cos_i, "pos"), (sin_i, "pos")],
                       tm=tm, name="proj_small")
        gates = matmul(nrm, w_gate, _ep_gate, bf16,
                       extras=[(gate_bias[l].reshape(1, 2 * d), "row")],
                       tm=tm, tn=WIDE_TN, name="proj_gate")

        ya = gmlp_gate(uv, gmlp_v_norm[l], gmlp_ws[l], gmlp_bs[l].T)
        yb = dsa_attention(q, kv, qi, small, q_norm[l], rope_a_t, rope_i_t, batch, seq)
        m = gated_merge(ya, yb, w_br_a[l].astype(bf16), w_br_b[l].astype(bf16), gates,
                        tm=tm, tn=WIDE_TN)
        h = matmul(m, w_out[l].astype(bf16), _ep_residual, f32, extras=[(h, "tile")],
                   tm=tm, tn=WIDE_TN, name="out_proj")

        h = _ffn(h, ffn2_norm[l], ffn2_w1[l], ffn2_w3[l], ffn2_w2[l], nk_down)
    return h.reshape(batch, seq, d)
```

```python
import functools
import math

import jax
import jax.numpy as jnp
import numpy as np
from jax import lax
from jax.experimental import pallas as pl
from jax.experimental.pallas import tpu as pltpu

CHUNK = 64
GROUP_DIM = 128
MIX_BLOCK = 128
HEAD_DIM = 128
IDX_DIM = 64
Q_BLOCK = 128
KEY_TILE = 128
SCORE_GROUP = 4
ATTN_GROUP = 4
TOPK_MAX = 256
ROPE_THETA = 10000.0
EPS = 1e-6
LANES = 128
BF16_SUBLANES = 16
NEG_BIG = -1e30
INT_MIN = -(2 ** 31)
VMEM_LIMIT_BYTES = 60000 * 1024
WIDE_TN = 1024

f32 = jnp.float32
bf16 = jnp.bfloat16


def _params(sem):
    return pltpu.CompilerParams(dimension_semantics=sem,
                                vmem_limit_bytes=VMEM_LIMIT_BYTES)


def _tile(dim, want):
    return dim if dim <= want else want


def _gelu(x):
    return 0.5 * x * (1.0 + lax.erf(x * (2.0 ** -0.5)))


def _rope_pairs(x, cos_t, sin_t, half):
    width = x.shape[-1]
    lane = lax.broadcasted_iota(jnp.int32, x.shape, x.ndim - 1)
    fwd = pltpu.roll(x, width - half, axis=x.ndim - 1)
    bwd = pltpu.roll(x, half, axis=x.ndim - 1)
    partner = jnp.where((lane % (2 * half)) < half, fwd, bwd)
    return x * cos_t + partner * sin_t


def _row_rsqrt(sumsq, width):
    return lax.rsqrt(sumsq * (1.0 / width) + EPS)


def _cast_sumsq_kernel(x_ref, o_ref, ss_ref):
    x = x_ref[...]
    o_ref[...] = x.astype(o_ref.dtype)
    ss_ref[...] = jnp.sum(x * x, axis=-1, keepdims=True)


def cast_sumsq(x, tm=256):
    n, d = x.shape
    tm = _tile(n, tm)
    return pl.pallas_call(
        _cast_sumsq_kernel,
        grid=(pl.cdiv(n, tm),),
        in_specs=[pl.BlockSpec((tm, d), lambda i: (i, 0))],
        out_specs=[pl.BlockSpec((tm, d), lambda i: (i, 0)),
                   pl.BlockSpec((tm, 1), lambda i: (i, 0))],
        out_shape=[jax.ShapeDtypeStruct((n, d), bf16),
                   jax.ShapeDtypeStruct((n, 1), f32)],
        compiler_params=_params(("parallel",)),
        name="cast_sumsq",
    )(x)


def _side_plan(sides, steps, flat_step):
    in_specs, in_arrays, out_specs, out_shapes, scaled = [], [], [], [], []
    for src, scale in sides:
        rows, cols = src.shape
        tr = BF16_SUBLANES * (-(-rows // (BF16_SUBLANES * steps)))
        n_tiles = -(-rows // tr)
        assert n_tiles <= steps
        idx = lambda *g, n_tiles=n_tiles: (jnp.minimum(flat_step(*g), n_tiles - 1), 0)
        in_specs.append(pl.BlockSpec((tr, cols), idx))
        in_arrays.append(src)
        if scale is not None:
            in_specs.append(pl.BlockSpec((tr, 1), idx))
            in_arrays.append(scale)
        scaled.append(scale is not None)
        out_specs.append(pl.BlockSpec((tr, cols), idx))
        out_shapes.append(jax.ShapeDtypeStruct((rows, cols), bf16))
    return in_specs, in_arrays, out_specs, out_shapes, tuple(scaled)


def _take_side_inputs(it, scaled):
    return [(next(it), next(it) if s else None) for s in scaled]


def _run_sides(side_in, side_out):
    for (src_ref, scale_ref), dst_ref in zip(side_in, side_out):
        v = src_ref[...]
        if scale_ref is not None:
            v = v * scale_ref[...]
        dst_ref[...] = v.astype(dst_ref.dtype)


def _swiglu_up_kernel(*refs, scaled):
    it = iter(refs)
    x_ref, ss_ref, w1_ref, w3_ref = next(it), next(it), next(it), next(it)
    side_in = _take_side_inputs(it, scaled)
    o_ref = next(it)
    side_out = [next(it) for _ in scaled]
    x = x_ref[...]
    r = _row_rsqrt(ss_ref[...], x.shape[1])
    a = jnp.dot(x, w1_ref[...], preferred_element_type=f32) * r
    b = jnp.dot(x, w3_ref[...], preferred_element_type=f32) * r
    o_ref[...] = (a * jax.nn.sigmoid(a) * b).astype(o_ref.dtype)
    _run_sides(side_in, side_out)


def swiglu_up(x16, sumsq, w1, w3, sides=(), tm=1024, tn=512):
    n, d = x16.shape
    f = w1.shape[1]
    tm, tn = _tile(n, tm), _tile(f, tn)
    ni, nj = pl.cdiv(n, tm), pl.cdiv(f, tn)
    s_in, s_arr, s_out, s_shape, scaled = _side_plan(sides, ni * nj, lambda i, j: i * nj + j)
    outs = pl.pallas_call(
        functools.partial(_swiglu_up_kernel, scaled=scaled),
        grid=(ni, nj),
        in_specs=[pl.BlockSpec((tm, d), lambda i, j: (i, 0)),
                  pl.BlockSpec((tm, 1), lambda i, j: (i, 0)),
                  pl.BlockSpec((d, tn), lambda i, j: (0, j)),
                  pl.BlockSpec((d, tn), lambda i, j: (0, j))] + s_in,
        out_specs=[pl.BlockSpec((tm, tn), lambda i, j: (i, j))] + s_out,
        out_shape=[jax.ShapeDtypeStruct((n, f), bf16)] + s_shape,
        compiler_params=_params(("arbitrary", "arbitrary")),
        name="swiglu_up",
    )(x16, sumsq, w1, w3, *s_arr)
    return outs[0], outs[1:]


def _mm_kernel(*refs, nk, kdim, n_extra, epilogue, row_norm, emit_norm, scaled):
    it = iter(refs)
    a_ref, w_ref = next(it), next(it)
    ss_in_ref = next(it) if row_norm else None
    extra = [next(it) for _ in range(n_extra)]
    side_in = _take_side_inputs(it, scaled)
    o_ref = next(it)
    o16_ref, ss_out_ref = (next(it), next(it)) if emit_norm else (None, None)
    side_out = [next(it) for _ in scaled]
    acc_ref = next(it) if nk > 1 else None

    def finish(acc):
        if row_norm:
            acc = acc * _row_rsqrt(ss_in_ref[...], kdim)
        out = epilogue(acc, *[e[...] for e in extra])
        o_ref[...] = out.astype(o_ref.dtype)
        if emit_norm:
            o16_ref[...] = out.astype(o16_ref.dtype)
            part_ss = jnp.sum(out * out, axis=-1, keepdims=True)
            j = pl.program_id(1)

            @pl.when(j == 0)
            def _():
                ss_out_ref[...] = part_ss

            @pl.when(j > 0)
            def _():
                ss_out_ref[...] += part_ss

    part = jnp.dot(a_ref[...], w_ref[...], preferred_element_type=f32)
    if nk == 1:
        finish(part)
    else:
        k = pl.program_id(2)

        @pl.when(k == 0)
        def _():
            acc_ref[...] = part

        @pl.when(k > 0)
        def _():
            acc_ref[...] += part

        @pl.when(k == nk - 1)
        def _():
            finish(acc_ref[...])

    _run_sides(side_in, side_out)


def matmul(a, w, epilogue, out_dtype, extras=(), tm=1024, tn=512, nk=1, name="mm",
           n_cols=None, col_off=0, row_sumsq=None, emit_norm=False, sides=()):
    m, kdim = a.shape
    n = w.shape[1] if n_cols is None else n_cols
    tm, tn = _tile(m, tm), _tile(n, tn)
    assert kdim % nk == 0 and col_off % tn == 0
    tk = kdim // nk
    assert nk == 1 or tk % LANES == 0
    ni, nj, joff = pl.cdiv(m, tm), pl.cdiv(n, tn), col_off // tn
    in_specs = [pl.BlockSpec((tm, tk), lambda i, j, k: (i, k)),
                pl.BlockSpec((tk, tn), lambda i, j, k: (k, j + joff))]
    arrays = []
    if row_sumsq is not None:
        in_specs.append(pl.BlockSpec((tm, 1), lambda i, j, k: (i, 0)))
        arrays.append(row_sumsq)
    n_extra = len(extras)
    for arr, kind in extras:
        arrays.append(arr)
        if kind == "tile":
            in_specs.append(pl.BlockSpec((tm, tn), lambda i, j, k: (i, j)))
        elif kind == "row":
            in_specs.append(pl.BlockSpec((1, tn), lambda i, j, k: (0, j)))
        elif kind == "const":
            in_specs.append(pl.BlockSpec(arr.shape, lambda i, j, k: (0, 0)))
        elif kind == "pos":
            assert arr.shape[0] % tm == 0
            nper = arr.shape[0] // tm
            in_specs.append(pl.BlockSpec((tm, arr.shape[1]),
                                         lambda i, j, k, nper=nper: (i % nper, 0)))
        else:
            raise ValueError(kind)
    s_in, s_arr, s_out, s_shape, scaled = _side_plan(
        sides, ni * nj * nk, lambda i, j, k: (i * nj + j) * nk + k)
    out_specs = [pl.BlockSpec((tm, tn), lambda i, j, k: (i, j))]
    out_shape = [jax.ShapeDtypeStruct((m, n), out_dtype)]
    if emit_norm:
        out_specs += [pl.BlockSpec((tm, tn), lambda i, j, k: (i, j)),
                      pl.BlockSpec((tm, 1), lambda i, j, k: (i, 0))]
        out_shape += [jax.ShapeDtypeStruct((m, n), bf16), jax.ShapeDtypeStruct((m, 1), f32)]
    scratch = [pltpu.VMEM((tm, tn), f32)] if nk > 1 else []
    outs = pl.pallas_call(
        functools.partial(_mm_kernel, nk=nk, kdim=kdim, n_extra=n_extra, epilogue=epilogue,
                          row_norm=row_sumsq is not None, emit_norm=emit_norm, scaled=scaled),
        grid=(ni, nj, nk),
        in_specs=in_specs + s_in,
        out_specs=out_specs + s_out,
        out_shape=out_shape + s_shape,
        scratch_shapes=scratch,
        compiler_params=_params(("arbitrary", "arbitrary", "arbitrary")),
        name=name,
    )(a, w, *arrays, *s_arr)
    return outs[0] if len(outs) == 1 else tuple(outs)


def _ep_half_residual(acc, res):
    return res + 0.5 * acc


def _ep_residual(acc, res):
    return res + acc


def _ep_gelu(acc):
    return _gelu(acc)


def _ep_gate(acc, bias):
    return jax.nn.sigmoid(acc + bias)


def _ep_identity(acc):
    return acc


def _ep_kv(acc, gain, cos_t, sin_t):
    k = acc[:, :HEAD_DIM]
    ms = jnp.mean(k * k, axis=-1, keepdims=True)
    k = _rope_pairs(k * lax.rsqrt(ms + EPS) * gain, cos_t, sin_t, HEAD_DIM // 2)
    return jnp.concatenate([k, acc[:, HEAD_DIM:]], axis=-1)


def _ep_small(acc, gain, cos_t, sin_t, *, idx_scale):
    lane = lax.broadcasted_iota(jnp.int32, acc.shape, 1)
    is_k = lane < IDX_DIM
    ms = jnp.sum(jnp.where(is_k, acc * acc, 0.0), axis=-1, keepdims=True) * (1.0 / IDX_DIM)
    k = _rope_pairs(acc * lax.rsqrt(ms + EPS) * gain, cos_t, sin_t, IDX_DIM // 2)
    return jnp.where(is_k, k, acc * idx_scale)


def _gmlp_kernel(u_ref, v_ref, gain_ref, ws_ref, bs_ref, o_ref):
    groups = ws_ref.shape[0]
    v = v_ref[...].astype(f32)
    ms = jnp.mean(v * v, axis=-1, keepdims=True)
    vn = (v * lax.rsqrt(ms + EPS) * gain_ref[...]).astype(bf16)
    row_chunk = lax.broadcasted_iota(jnp.int32, (MIX_BLOCK, MIX_BLOCK), 0) // CHUNK
    col_chunk = lax.broadcasted_iota(jnp.int32, (MIX_BLOCK, MIX_BLOCK), 1) // CHUNK
    causal = col_chunk <= row_chunk
    bs = bs_ref[...]
    for g in range(groups):
        sl = slice(g * GROUP_DIM, (g + 1) * GROUP_DIM)
        w = jnp.where(causal, ws_ref[g], 0.0).astype(bf16)
        mixed = jnp.dot(w, vn[:, sl], preferred_element_type=f32) + bs[:, g:g + 1]
        o_ref[:, sl] = (u_ref[:, sl].astype(f32) * mixed).astype(o_ref.dtype)


def gmlp_gate(uv, gain, ws, bs_t):
    n = uv.shape[0]
    gw = uv.shape[1] // 2
    groups = gw // GROUP_DIM
    return pl.pallas_call(
        _gmlp_kernel,
        grid=(n // MIX_BLOCK,),
        in_specs=[pl.BlockSpec((MIX_BLOCK, gw), lambda i: (i, 0)),
                  pl.BlockSpec((MIX_BLOCK, gw), lambda i: (i, 1)),
                  pl.BlockSpec((1, gw), lambda i: (0, 0)),
                  pl.BlockSpec((groups, MIX_BLOCK, MIX_BLOCK), lambda i: (0, 0, 0)),
                  pl.BlockSpec((MIX_BLOCK, groups), lambda i: (0, 0))],
        out_specs=pl.BlockSpec((MIX_BLOCK, gw), lambda i: (i, 0)),
        out_shape=jax.ShapeDtypeStruct((n, gw), bf16),
        compiler_params=_params(("parallel",)),
        name="gmlp_gate",
    )(uv, uv, gain.reshape(1, gw), ws, bs_t)


def _sortable(x):
    bits = lax.bitcast_convert_type(x, jnp.int32)
    return bits ^ ((bits >> 31) & jnp.int32(0x7FFFFFFF))


def _dsa_kernel(q_ref, kv_ref, qi_ref, smallk_ref, smallq_ref, qgain_ref,
                cos_a_ref, sin_a_ref, cos_i_ref, sin_i_ref, o_ref,
                qit_ref, q2t_ref, vt_ref, keys_ref, m_ref, l_ref, acc_ref,
                *, n_qblk, top):
    qb = pl.program_id(0) % n_qblk
    n_kt = qb + 1
    n_grp = (n_kt + SCORE_GROUP - 1) // SCORE_GROUP
    n_grp_attn = (n_kt + ATTN_GROUP - 1) // ATTN_GROUP
    heads = q_ref.shape[1] // HEAD_DIM
    idx_heads = qi_ref.shape[1] // IDX_DIM

    @pl.when(qb == 0)
    def _():
        def vt_tile(kt, carry):
            base = pl.multiple_of(kt * KEY_TILE, KEY_TILE)
            v_t = kv_ref[pl.ds(base, KEY_TILE), :][:, HEAD_DIM:]
            vt_ref[kt] = jnp.transpose(v_t.astype(f32)).astype(bf16)
            return carry

        lax.fori_loop(0, n_qblk * Q_BLOCK // KEY_TILE, vt_tile, 0)

    def rope_rows(x, cos_t, sin_t):
        half = x.shape[0] // 2
        x1, x2 = x[:half], x[half:]
        return jnp.concatenate([x1 * cos_t - x2 * sin_t, x1 * sin_t + x2 * cos_t], axis=0)

    cos_i, sin_i = cos_i_ref[...], sin_i_ref[...]
    for c in range(qi_ref.shape[1] // LANES):
        t = jnp.transpose(qi_ref[:, c * LANES:(c + 1) * LANES])
        for j in range(LANES // IDX_DIM):
            h = c * (LANES // IDX_DIM) + j
            qit_ref[:, h * Q_BLOCK:(h + 1) * Q_BLOCK] = rope_rows(
                t[j * IDX_DIM:(j + 1) * IDX_DIM], cos_i, sin_i).astype(bf16)
    cos_a, sin_a, qgain = cos_a_ref[...], sin_a_ref[...], qgain_ref[...]
    for h in range(heads):
        hs = slice(h * HEAD_DIM, (h + 1) * HEAD_DIM)
        t = jnp.transpose(q_ref[:, hs])
        ms = jnp.mean(t * t, axis=0, keepdims=True)
        t = rope_rows(t * lax.rsqrt(ms + EPS) * qgain, cos_a, sin_a)
        q2t_ref[:, hs] = (t * (HEAD_DIM ** -0.5)).astype(bf16)
    w_t = jnp.transpose(smallq_ref[...])
    q_chunk = (qb * Q_BLOCK + lax.broadcasted_iota(jnp.int32, (KEY_TILE, Q_BLOCK), 1)) // CHUNK
    key_off = lax.broadcasted_iota(jnp.int32, (KEY_TILE, Q_BLOCK), 0)

    def score_group(g, carry):
        rows = SCORE_GROUP * KEY_TILE
        base = pl.multiple_of(g * rows, rows)
        ki = smallk_ref[pl.ds(base, rows), :][:, :IDX_DIM].astype(bf16)
        rel = jnp.dot(ki, qit_ref[...], preferred_element_type=f32)
        for u in range(SCORE_GROUP):
            us = slice(u * KEY_TILE, (u + 1) * KEY_TILE)
            s = jnp.zeros((KEY_TILE, Q_BLOCK), f32)
            for h in range(idx_heads):
                r = jnp.maximum(rel[us, h * Q_BLOCK:(h + 1) * Q_BLOCK], 0.0)
                s = s + w_t[IDX_DIM + h:IDX_DIM + h + 1, :] * r
            adm = (base + u * KEY_TILE + key_off) // CHUNK <= q_chunk
            keys_ref[g * SCORE_GROUP + u] = jnp.where(adm, _sortable(s), INT_MIN)
        return carry

    lax.fori_loop(0, n_grp, score_group, 0)

    def count_keys(pred):
        def count_group(g, cnt):
            for u in range(SCORE_GROUP):
                kt = g * SCORE_GROUP + u
                hit = pred(keys_ref[kt], kt * KEY_TILE + key_off).astype(jnp.int32)
                cnt = cnt + jnp.sum(hit.reshape(KEY_TILE // 8, 8, Q_BLOCK), axis=0)
            return cnt

        cnt = lax.fori_loop(0, n_grp, count_group, jnp.zeros((8, Q_BLOCK), jnp.int32))
        return jnp.sum(cnt, axis=0, keepdims=True)

    def bit_step(i, carry):
        prefix, n_ge = carry
        cand = prefix + jnp.left_shift(jnp.int32(1), 31 - i)
        total = count_keys(lambda kk, idx: kk >= cand)
        take = total >= top
        return jnp.where(take, cand, prefix), jnp.where(take, total, n_ge)

    thr, n_ge = lax.fori_loop(
        0, 32, bit_step,
        (jnp.full((1, Q_BLOCK), INT_MIN, jnp.int32), jnp.zeros((1, Q_BLOCK), jnp.int32)))

    def tie_search(_):
        def idx_step(i, bound):
            cand = bound + jnp.left_shift(jnp.int32(1), idx_bits - 1 - i)
            total = count_keys(lambda kk, idx: (kk > thr) | ((kk == thr) & (idx < cand)))
            return jnp.where(total < top, cand, bound)

        return lax.fori_loop(0, idx_bits, idx_step, jnp.zeros((1, Q_BLOCK), jnp.int32))

    idx_bits = int(keys_ref.shape[0] * KEY_TILE).bit_length()
    no_ties = lambda _: jnp.full((1, Q_BLOCK), keys_ref.shape[0] * KEY_TILE, jnp.int32)
    last_idx = lax.cond(jnp.max(n_ge) > top, tie_search, no_ties, 0)

    m_ref[...] = jnp.full(m_ref.shape, NEG_BIG, f32)
    l_ref[...] = jnp.zeros(l_ref.shape, f32)
    acc_ref[...] = jnp.zeros(acc_ref.shape, f32)

    def attn_group(g, carry):
        rows = ATTN_GROUP * KEY_TILE
        base = pl.multiple_of(g * rows, rows)
        k_t = kv_ref[pl.ds(base, rows), :][:, :HEAD_DIM]
        logits = jnp.dot(k_t, q2t_ref[...], preferred_element_type=f32)
        sel = []
        for u in range(ATTN_GROUP):
            kk = keys_ref[g * ATTN_GROUP + u]
            idx = base + u * KEY_TILE + key_off
            sel.append(((kk > thr) | ((kk == thr) & (idx <= last_idx))) & (kk != INT_MIN))
        sel = jnp.concatenate(sel, axis=0)
        probs, alphas = [], []
        for h in range(heads):
            hs = slice(h * Q_BLOCK, (h + 1) * Q_BLOCK)
            lg = jnp.where(sel, logits[:, hs], NEG_BIG)
            m_old = m_ref[:, hs]
            m_new = jnp.maximum(m_old, jnp.max(lg, axis=0, keepdims=True))
            alpha = jnp.exp(m_old - m_new)
            p = jnp.exp(lg - m_new)
            l_ref[:, hs] = alpha * l_ref[:, hs] + jnp.sum(p, axis=0, keepdims=True)
            m_ref[:, hs] = m_new
            probs.append(p.astype(bf16))
            alphas.append(alpha)
        v_t = jnp.concatenate([vt_ref[g * ATTN_GROUP + u] for u in range(ATTN_GROUP)], axis=1)
        pv = jnp.dot(v_t, jnp.concatenate(probs, axis=1),
                     preferred_element_type=f32)
        acc_ref[...] = acc_ref[...] * jnp.concatenate(alphas, axis=1) + pv
        return carry

    lax.fori_loop(0, n_grp_attn, attn_group, 0)

    out = acc_ref[...] / l_ref[...]
    for h in range(heads):
        hs = slice(h * HEAD_DIM, (h + 1) * HEAD_DIM)
        o_ref[:, hs] = jnp.transpose(out[:, hs]).astype(o_ref.dtype)


def dsa_attention(q, kv, qi, small, q_gain, rope_a, rope_i, batch, seq):
    n, aw = q.shape
    n_qblk = seq // Q_BLOCK
    n_kt = seq // KEY_TILE
    assert n_kt % SCORE_GROUP == 0 and SCORE_GROUP % ATTN_GROUP == 0
    heads = aw // HEAD_DIM
    iw = qi.shape[1]
    top = min(TOPK_MAX, seq // 4)
    q_gain_t = jnp.broadcast_to(q_gain.reshape(HEAD_DIM, 1), (HEAD_DIM, Q_BLOCK))
    per_qblk = lambda rows: pl.BlockSpec((rows, Q_BLOCK), lambda g: (0, g % n_qblk))
    return pl.pallas_call(
        functools.partial(_dsa_kernel, n_qblk=n_qblk, top=top),
        grid=(batch * n_qblk,),
        in_specs=[pl.BlockSpec((Q_BLOCK, aw), lambda g: (g, 0)),
                  pl.BlockSpec((seq, 2 * HEAD_DIM), lambda g: (g // n_qblk, 0)),
                  pl.BlockSpec((Q_BLOCK, iw), lambda g: (g, 0)),
                  pl.BlockSpec((seq, LANES), lambda g: (g // n_qblk, 0)),
                  pl.BlockSpec((Q_BLOCK, LANES), lambda g: (g, 0)),
                  pl.BlockSpec((HEAD_DIM, Q_BLOCK), lambda g: (0, 0)),
                  per_qblk(HEAD_DIM // 2), per_qblk(HEAD_DIM // 2),
                  per_qblk(IDX_DIM // 2), per_qblk(IDX_DIM // 2)],
        out_specs=pl.BlockSpec((Q_BLOCK, aw), lambda g: (g, 0)),
        out_shape=jax.ShapeDtypeStruct((n, aw), bf16),
        scratch_shapes=[pltpu.VMEM((IDX_DIM, (iw // IDX_DIM) * Q_BLOCK), bf16),
                        pltpu.VMEM((HEAD_DIM, heads * Q_BLOCK), bf16),
                        pltpu.VMEM((n_kt, HEAD_DIM, KEY_TILE), bf16),
                        pltpu.VMEM((n_kt, KEY_TILE, Q_BLOCK), jnp.int32),
                        pltpu.VMEM((1, heads * Q_BLOCK), f32),
                        pltpu.VMEM((1, heads * Q_BLOCK), f32),
                        pltpu.VMEM((HEAD_DIM, heads * Q_BLOCK), f32)],
        compiler_params=_params(("arbitrary",)),
        name="dsa_attention",
    )(q, kv, qi, small, small, q_gain_t, *rope_a, *rope_i)


def _merge_kernel(ya_ref, yb_ref, wa_ref, wb_ref, ga_ref, gb_ref, o_ref):
    a = jnp.dot(ya_ref[...], wa_ref[...], preferred_element_type=f32)
    b = jnp.dot(yb_ref[...], wb_ref[...], preferred_element_type=f32)
    o_ref[...] = (ga_ref[...].astype(f32) * a + gb_ref[...].astype(f32) * b).astype(o_ref.dtype)


def gated_merge(ya, yb, wa, wb, gates, tm=1024, tn=512):
    n, ka = ya.shape
    kb = yb.shape[1]
    d = wa.shape[1]
    tm, tn = _tile(n, tm), _tile(d, tn)
    nj = d // tn
    assert d % tn == 0
    return pl.pallas_call(
        _merge_kernel,
        grid=(pl.cdiv(n, tm), nj),
        in_specs=[pl.BlockSpec((tm, ka), lambda i, j: (i, 0)),
                  pl.BlockSpec((tm, kb), lambda i, j: (i, 0)),
                  pl.BlockSpec((ka, tn), lambda i, j: (0, j)),
                  pl.BlockSpec((kb, tn), lambda i, j: (0, j)),
                  pl.BlockSpec((tm, tn), lambda i, j: (i, j)),
                  pl.BlockSpec((tm, tn), lambda i, j: (i, j + nj))],
        out_specs=pl.BlockSpec((tm, tn), lambda i, j: (i, j)),
        out_shape=jax.ShapeDtypeStruct((n, d), bf16),
        compiler_params=_params(("parallel", "arbitrary")),
        name="gated_merge",
    )(ya, yb, wa, wb, gates, gates)


def _rope_tables(seq, dim, lanes):
    inv = ROPE_THETA ** (-jnp.arange(0, dim, 2, dtype=f32) / dim)
    ang = jnp.arange(seq, dtype=f32)[:, None] * inv[None, :]
    cos, sin = jnp.cos(ang), jnp.sin(ang)
    reps = lanes // dim
    cos_t = jnp.tile(jnp.concatenate([cos, cos], axis=-1), (1, reps))
    sin_t = jnp.tile(jnp.concatenate([-sin, sin], axis=-1), (1, reps))
    return cos_t, sin_t


def _rope_tables_t(seq, dim):
    inv = ROPE_THETA ** (-jnp.arange(0, dim, 2, dtype=f32) / dim)
    ang = inv[:, None] * jnp.arange(seq, dtype=f32)[None, :]
    return jnp.cos(ang), jnp.sin(ang)


def _fold_gain(gain, w):
    return (gain[:, None] * w).astype(bf16)


def _project(h16, h_ss, w16, col_off, n_cols, epilogue, out_dtype, tn, **kw):
    tn = min(tn, n_cols)
    if col_off % tn == 0 and n_cols % tn == 0:
        return matmul(h16, w16, epilogue, out_dtype, tn=tn, n_cols=n_cols, col_off=col_off,
                      row_sumsq=h_ss, **kw)
    return matmul(h16, w16[:, col_off:col_off + n_cols], epilogue, out_dtype, tn=tn,
                  row_sumsq=h_ss, **kw)


def kernel(x, ffn1_norm, ffn1_w1, ffn1_w3, ffn1_w2, mix_norm, w_in, gate_bias,
           gmlp_v_norm, gmlp_ws, gmlp_bs, q_norm, k_norm, idx_k_norm,
           w_br_a, w_br_b, w_out, ffn2_norm, ffn2_w1, ffn2_w3, ffn2_w2):
    batch, seq, d = x.shape
    depth = ffn1_norm.shape[0]
    n = batch * seq
    gw = gmlp_v_norm.shape[1]
    aw = w_br_b.shape[1]
    idx_heads = d // 128
    iw = idx_heads * IDX_DIM
    d_ff = ffn1_w1.shape[2]
    nk_down = 2 if (d_ff % (2 * LANES) == 0 and d_ff > 4096) else 1
    c_uv, c_q, c_kv, c_qi = 2 * gw, aw, 2 * HEAD_DIM, iw
    o_q = c_uv
    o_kv = o_q + c_q
    o_qi = o_kv + c_kv
    o_small = o_qi + c_qi
    o_gate = o_small + IDX_DIM + idx_heads
    assert o_gate + 2 * d == w_in.shape[2]
    assert IDX_DIM + idx_heads <= LANES
    idx_scale = (idx_heads ** -0.5) * (IDX_DIM ** -0.5)

    cos_a, sin_a = _rope_tables(seq, HEAD_DIM, LANES)
    cos_i, sin_i = _rope_tables(seq, IDX_DIM, LANES)
    rope_a_t = _rope_tables_t(seq, HEAD_DIM)
    rope_i_t = _rope_tables_t(seq, IDX_DIM)
    tm = _tile(seq, 1024)
    assert seq % tm == 0 and seq % Q_BLOCK == 0

    h = x.reshape(n, d)
    h16, h_ss = cast_sumsq(h)
    for l in range(depth):
        g, (w2_16, w_in16) = swiglu_up(
            h16, h_ss, _fold_gain(ffn1_norm[l], ffn1_w1[l]), _fold_gain(ffn1_norm[l], ffn1_w3[l]),
            sides=[(ffn1_w2[l], None), (w_in[l], mix_norm[l].reshape(d, 1))])
        h, h16, h_ss, wa16, wb16, wo16 = matmul(
            g, w2_16, _ep_half_residual, f32, extras=[(h, "tile")], nk=nk_down, emit_norm=True,
            sides=[(w_br_a[l], None), (w_br_b[l], None), (w_out[l], None)], name="ffn_down")

        w_qi = w_in16[:, o_qi:o_small]
        w_small = jnp.pad(w_in16[:, o_small:o_gate], ((0, 0), (0, LANES - (o_gate - o_small))))
        w_gate = w_in16[:, o_gate:]
        ffn2_gain = ffn2_norm[l].reshape(d, 1)
        uv, w2b_16 = _project(h16, h_ss, w_in16, 0, c_uv, _ep_gelu, bf16, WIDE_TN, tm=tm,
                              sides=[(ffn2_w2[l], None)], name="proj_uv")
        q = _project(h16, h_ss, w_in16, o_q, c_q, _ep_identity, f32, WIDE_TN, tm=tm, name="proj_q")
        kv = _project(h16, h_ss, w_in16, o_kv, c_kv, _ep_kv, bf16, c_kv, tm=tm,
                      extras=[(k_norm[l].reshape(1, HEAD_DIM), "const"),
                              (cos_a, "pos"), (sin_a, "pos")], name="proj_kv")
        qi = matmul(h16, w_qi, _ep_identity, f32, tm=tm, tn=WIDE_TN, row_sumsq=h_ss,
                    name="proj_qi")
        k_gain = jnp.tile(idx_k_norm[l].reshape(1, IDX_DIM), (1, LANES // IDX_DIM))
        small = matmul(h16, w_small, functools.partial(_ep_small, idx_scale=idx_scale), f32,
                       extras=[(k_gain, "const"), (cos_i, "pos"), (sin_i, "pos")],
                       tm=tm, row_sumsq=h_ss, name="proj_small")
        gates, w1b_16, w3b_16 = matmul(
            h16, w_gate, _ep_gate, bf16, extras=[(gate_bias[l].reshape(1, 2 * d), "row")],
            tm=tm, tn=WIDE_TN, row_sumsq=h_ss,
            sides=[(ffn2_w1[l], ffn2_gain), (ffn2_w3[l], ffn2_gain)], name="proj_gate")

        ya = gmlp_gate(uv, gmlp_v_norm[l], gmlp_ws[l], gmlp_bs[l].T)
        yb = dsa_attention(q, kv, qi, small, q_norm[l], rope_a_t, rope_i_t, batch, seq)
        m = gated_merge(ya, yb, wa16, wb16, gates, tm=tm, tn=WIDE_TN)
        h, h16, h_ss = matmul(m, wo16, _ep_residual, f32, extras=[(h, "tile")],
                              tm=tm, emit_norm=True, name="out_proj")

        g, _ = swiglu_up(h16, h_ss, w1b_16, w3b_16)
        last = l == depth - 1
        out = matmul(g, w2b_16, _ep_half_residual, f32, extras=[(h, "tile")], nk=nk_down,
                     emit_norm=not last, name="ffn_down")
        h, h16, h_ss = (out, None, None) if last else out
    return h.reshape(batch, seq, d)
```

```python
import functools
import math

import jax
import jax.numpy as jnp
import numpy as np
from jax import lax
from jax.experimental import pallas as pl
from jax.experimental.pallas import tpu as pltpu

CHUNK = 64
GROUP_DIM = 128
MIX_BLOCK = 128
HEAD_DIM = 128
IDX_DIM = 64
Q_BLOCK = 128
KEY_TILE = 128
SCORE_GROUP = 4
ATTN_GROUP = 4
TOPK_MAX = 256
ROPE_THETA = 10000.0
EPS = 1e-6
LANES = 128
BF16_SUBLANES = 16
NEG_BIG = -1e30
INT_MIN = -(2 ** 31)
VMEM_LIMIT_BYTES = 60000 * 1024
WIDE_TN = 1024

f32 = jnp.float32
bf16 = jnp.bfloat16


def _params(sem):
    return pltpu.CompilerParams(dimension_semantics=sem,
                                vmem_limit_bytes=VMEM_LIMIT_BYTES)


def _tile(dim, want):
    return dim if dim <= want else want


def _gelu(x):
    return 0.5 * x * (1.0 + lax.erf(x * (2.0 ** -0.5)))


def _rope_pairs(x, cos_t, sin_t, half):
    width = x.shape[-1]
    lane = lax.broadcasted_iota(jnp.int32, x.shape, x.ndim - 1)
    fwd = pltpu.roll(x, width - half, axis=x.ndim - 1)
    bwd = pltpu.roll(x, half, axis=x.ndim - 1)
    partner = jnp.where((lane % (2 * half)) < half, fwd, bwd)
    return x * cos_t + partner * sin_t


def _row_rsqrt(sumsq, width):
    return lax.rsqrt(sumsq * (1.0 / width) + EPS)


def _cast_sumsq_kernel(x_ref, o_ref, ss_ref):
    x = x_ref[...]
    o_ref[...] = x.astype(o_ref.dtype)
    ss_ref[...] = jnp.sum(x * x, axis=-1, keepdims=True)


def cast_sumsq(x, tm=256):
    n, d = x.shape
    tm = _tile(n, tm)
    return pl.pallas_call(
        _cast_sumsq_kernel,
        grid=(pl.cdiv(n, tm),),
        in_specs=[pl.BlockSpec((tm, d), lambda i: (i, 0))],
        out_specs=[pl.BlockSpec((tm, d), lambda i: (i, 0)),
                   pl.BlockSpec((tm, 1), lambda i: (i, 0))],
        out_shape=[jax.ShapeDtypeStruct((n, d), bf16),
                   jax.ShapeDtypeStruct((n, 1), f32)],
        compiler_params=_params(("parallel",)),
        name="cast_sumsq",
    )(x)


def _side_plan(sides, steps, flat_step):
    in_specs, in_arrays, out_specs, out_shapes, scaled = [], [], [], [], []
    for src, scale in sides:
        rows, cols = src.shape
        tr = BF16_SUBLANES * (-(-rows // (BF16_SUBLANES * steps)))
        n_tiles = -(-rows // tr)
        assert n_tiles <= steps
        idx = lambda *g, n_tiles=n_tiles: (jnp.minimum(flat_step(*g), n_tiles - 1), 0)
        in_specs.append(pl.BlockSpec((tr, cols), idx))
        in_arrays.append(src)
        if scale is not None:
            if scale.shape == (1, cols):
                in_specs.append(pl.BlockSpec((1, cols), lambda *g: (0, 0)))
            else:
                assert scale.shape == (rows, 1)
                in_specs.append(pl.BlockSpec((tr, 1), idx))
            in_arrays.append(scale)
        scaled.append(scale is not None)
        out_specs.append(pl.BlockSpec((tr, cols), idx))
        out_shapes.append(jax.ShapeDtypeStruct((rows, cols), bf16))
    return in_specs, in_arrays, out_specs, out_shapes, tuple(scaled)


def _take_side_inputs(it, scaled):
    return [(next(it), next(it) if s else None) for s in scaled]


def _run_sides(side_in, side_out):
    for (src_ref, scale_ref), dst_ref in zip(side_in, side_out):
        v = src_ref[...]
        if scale_ref is not None:
            v = v * scale_ref[...]
        dst_ref[...] = v.astype(dst_ref.dtype)


def _swiglu_up_kernel(*refs, scaled):
    it = iter(refs)
    x_ref, ss_ref, w1_ref, w3_ref = next(it), next(it), next(it), next(it)
    side_in = _take_side_inputs(it, scaled)
    o_ref = next(it)
    side_out = [next(it) for _ in scaled]
    x = x_ref[...]
    r = _row_rsqrt(ss_ref[...], x.shape[1])
    a = jnp.dot(x, w1_ref[...], preferred_element_type=f32) * r
    b = jnp.dot(x, w3_ref[...], preferred_element_type=f32) * r
    o_ref[...] = (a * jax.nn.sigmoid(a) * b).astype(o_ref.dtype)
    _run_sides(side_in, side_out)


def swiglu_up(x16, sumsq, w1, w3, sides=(), tm=1024, tn=512):
    n, d = x16.shape
    f = w1.shape[1]
    tm, tn = _tile(n, tm), _tile(f, tn)
    ni, nj = pl.cdiv(n, tm), pl.cdiv(f, tn)
    s_in, s_arr, s_out, s_shape, scaled = _side_plan(sides, ni * nj, lambda i, j: i * nj + j)
    outs = pl.pallas_call(
        functools.partial(_swiglu_up_kernel, scaled=scaled),
        grid=(ni, nj),
        in_specs=[pl.BlockSpec((tm, d), lambda i, j: (i, 0)),
                  pl.BlockSpec((tm, 1), lambda i, j: (i, 0)),
                  pl.BlockSpec((d, tn), lambda i, j: (0, j)),
                  pl.BlockSpec((d, tn), lambda i, j: (0, j))] + s_in,
        out_specs=[pl.BlockSpec((tm, tn), lambda i, j: (i, j))] + s_out,
        out_shape=[jax.ShapeDtypeStruct((n, f), bf16)] + s_shape,
        compiler_params=_params(("arbitrary", "arbitrary")),
        name="swiglu_up",
    )(x16, sumsq, w1, w3, *s_arr)
    return outs[0], outs[1:]


def _mm_kernel(*refs, nk, kdim, n_extra, epilogue, row_norm, emit_norm, scaled, w_rows):
    it = iter(refs)
    a_ref, w_ref = next(it), next(it)
    ss_in_ref = next(it) if row_norm else None
    extra = [next(it) for _ in range(n_extra)]
    side_in = _take_side_inputs(it, scaled)
    o_ref = next(it)
    o16_ref, ss_out_ref = (next(it), next(it)) if emit_norm else (None, None)
    side_out = [next(it) for _ in scaled]
    acc_ref = next(it) if nk > 1 else None
    j, k = pl.program_id(1), pl.program_id(2)

    if nk > 1:
        @pl.when(k == 0)
        def _():
            acc_ref[...] = jnp.zeros(acc_ref.shape, f32)

    if emit_norm:
        @pl.when((j == 0) & (k == 0))
        def _():
            ss_out_ref[...] = jnp.zeros(ss_out_ref.shape, f32)

    dims = (((1,), (1,)), ((), ())) if w_rows else (((1,), (0,)), ((), ()))
    acc = lax.dot_general(a_ref[...], w_ref[...], dims, preferred_element_type=f32)
    if nk > 1:
        acc = acc_ref[...] + acc
        acc_ref[...] = acc
    if row_norm:
        acc = acc * _row_rsqrt(ss_in_ref[...], kdim)
    out = epilogue(acc, *[e[...] for e in extra])
    o_ref[...] = out.astype(o_ref.dtype)
    if emit_norm:
        o16_ref[...] = out.astype(o16_ref.dtype)
        part_ss = jnp.sum(out * out, axis=-1, keepdims=True)
        ss_out_ref[...] += jnp.where(k == nk - 1, part_ss, 0.0)

    _run_sides(side_in, side_out)


def matmul(a, w, epilogue, out_dtype, extras=(), tm=1024, tn=512, nk=1, name="mm",
           n_cols=None, col_off=0, row_sumsq=None, emit_norm=False, sides=(), w_rows=False):
    m, kdim = a.shape
    n = w.shape[0 if w_rows else 1] if n_cols is None else n_cols
    tm, tn = _tile(m, tm), _tile(n, tn)
    assert kdim % nk == 0 and col_off % tn == 0
    tk = kdim // nk
    assert nk == 1 or tk % LANES == 0
    ni, nj, joff = pl.cdiv(m, tm), pl.cdiv(n, tn), col_off // tn
    in_specs = [pl.BlockSpec((tm, tk), lambda i, j, k: (i, k)),
                pl.BlockSpec((tn, tk), lambda i, j, k: (j + joff, k)) if w_rows else
                pl.BlockSpec((tk, tn), lambda i, j, k: (k, j + joff))]
    arrays = []
    if row_sumsq is not None:
        in_specs.append(pl.BlockSpec((tm, 1), lambda i, j, k: (i, 0)))
        arrays.append(row_sumsq)
    n_extra = len(extras)
    for arr, kind in extras:
        arrays.append(arr)
        if kind == "tile":
            in_specs.append(pl.BlockSpec((tm, tn), lambda i, j, k: (i, j)))
        elif kind == "row":
            in_specs.append(pl.BlockSpec((1, tn), lambda i, j, k: (0, j)))
        elif kind == "const":
            in_specs.append(pl.BlockSpec(arr.shape, lambda i, j, k: (0, 0)))
        elif kind == "pos":
            assert arr.shape[0] % tm == 0
            nper = arr.shape[0] // tm
            in_specs.append(pl.BlockSpec((tm, arr.shape[1]),
                                         lambda i, j, k, nper=nper: (i % nper, 0)))
        else:
            raise ValueError(kind)
    s_in, s_arr, s_out, s_shape, scaled = _side_plan(
        sides, ni * nj * nk, lambda i, j, k: (i * nj + j) * nk + k)
    out_specs = [pl.BlockSpec((tm, tn), lambda i, j, k: (i, j))]
    out_shape = [jax.ShapeDtypeStruct((m, n), out_dtype)]
    if emit_norm:
        out_specs += [pl.BlockSpec((tm, tn), lambda i, j, k: (i, j)),
                      pl.BlockSpec((tm, 1), lambda i, j, k: (i, 0))]
        out_shape += [jax.ShapeDtypeStruct((m, n), bf16), jax.ShapeDtypeStruct((m, 1), f32)]
    scratch = [pltpu.VMEM((tm, tn), f32)] if nk > 1 else []
    outs = pl.pallas_call(
        functools.partial(_mm_kernel, nk=nk, kdim=kdim, n_extra=n_extra, epilogue=epilogue,
                          row_norm=row_sumsq is not None, emit_norm=emit_norm, scaled=scaled,
                          w_rows=w_rows),
        grid=(ni, nj, nk),
        in_specs=in_specs + s_in,
        out_specs=out_specs + s_out,
        out_shape=out_shape + s_shape,
        scratch_shapes=scratch,
        compiler_params=_params(("arbitrary", "arbitrary", "arbitrary")),
        name=name,
    )(a, w, *arrays, *s_arr)
    return outs[0] if len(outs) == 1 else tuple(outs)


def _ep_half_residual(acc, res):
    return res + 0.5 * acc


def _ep_residual(acc, res):
    return res + acc


def _ep_gelu(acc):
    return _gelu(acc)


def _ep_gate(acc, bias):
    return jax.nn.sigmoid(acc + bias)


def _ep_identity(acc):
    return acc


def _ep_kv(acc, gain, cos_t, sin_t):
    k = acc[:, :HEAD_DIM]
    ms = jnp.mean(k * k, axis=-1, keepdims=True)
    k = _rope_pairs(k * lax.rsqrt(ms + EPS) * gain, cos_t, sin_t, HEAD_DIM // 2)
    return jnp.concatenate([k, acc[:, HEAD_DIM:]], axis=-1)


def _ep_small(acc, gain, cos_t, sin_t, *, idx_scale):
    lane = lax.broadcasted_iota(jnp.int32, acc.shape, 1)
    is_k = lane < IDX_DIM
    ms = jnp.sum(jnp.where(is_k, acc * acc, 0.0), axis=-1, keepdims=True) * (1.0 / IDX_DIM)
    k = _rope_pairs(acc * lax.rsqrt(ms + EPS) * gain, cos_t, sin_t, IDX_DIM // 2)
    return jnp.where(is_k, k, acc * idx_scale)


def _gmlp_kernel(u_ref, v_ref, gain_ref, ws_ref, bs_ref, o_ref):
    groups = ws_ref.shape[0]
    v = v_ref[...].astype(f32)
    ms = jnp.mean(v * v, axis=-1, keepdims=True)
    vn = (v * lax.rsqrt(ms + EPS) * gain_ref[...]).astype(bf16)
    row_chunk = lax.broadcasted_iota(jnp.int32, (MIX_BLOCK, MIX_BLOCK), 0) // CHUNK
    col_chunk = lax.broadcasted_iota(jnp.int32, (MIX_BLOCK, MIX_BLOCK), 1) // CHUNK
    causal = col_chunk <= row_chunk
    bs = bs_ref[...]
    for g in range(groups):
        sl = slice(g * GROUP_DIM, (g + 1) * GROUP_DIM)
        w = jnp.where(causal, ws_ref[g], 0.0).astype(bf16)
        mixed = jnp.dot(w, vn[:, sl], preferred_element_type=f32) + bs[:, g:g + 1]
        o_ref[:, sl] = (u_ref[:, sl].astype(f32) * mixed).astype(o_ref.dtype)


def gmlp_gate(uv, gain, ws, bs_t):
    n = uv.shape[0]
    gw = uv.shape[1] // 2
    groups = gw // GROUP_DIM
    return pl.pallas_call(
        _gmlp_kernel,
        grid=(n // MIX_BLOCK,),
        in_specs=[pl.BlockSpec((MIX_BLOCK, gw), lambda i: (i, 0)),
                  pl.BlockSpec((MIX_BLOCK, gw), lambda i: (i, 1)),
                  pl.BlockSpec((1, gw), lambda i: (0, 0)),
                  pl.BlockSpec((groups, MIX_BLOCK, MIX_BLOCK), lambda i: (0, 0, 0)),
                  pl.BlockSpec((MIX_BLOCK, groups), lambda i: (0, 0))],
        out_specs=pl.BlockSpec((MIX_BLOCK, gw), lambda i: (i, 0)),
        out_shape=jax.ShapeDtypeStruct((n, gw), bf16),
        compiler_params=_params(("parallel",)),
        name="gmlp_gate",
    )(uv, uv, gain.reshape(1, gw), ws, bs_t)


def _sortable(x):
    bits = lax.bitcast_convert_type(x, jnp.int32)
    return bits ^ ((bits >> 31) & jnp.int32(0x7FFFFFFF))


def _dsa_kernel(q_ref, kv_ref, qi_ref, smallk_ref, smallq_ref, qgain_ref,
                cos_a_ref, sin_a_ref, cos_i_ref, sin_i_ref, o_ref,
                qit_ref, q2t_ref, vt_ref, keys_ref, m_ref, l_ref, acc_ref,
                *, n_qblk, top):
    qb = pl.program_id(0) % n_qblk
    n_kt = qb + 1
    n_grp = (n_kt + SCORE_GROUP - 1) // SCORE_GROUP
    n_grp_attn = (n_kt + ATTN_GROUP - 1) // ATTN_GROUP
    heads = q_ref.shape[1] // HEAD_DIM
    idx_heads = qi_ref.shape[1] // IDX_DIM

    @pl.when(qb == 0)
    def _():
        def vt_tile(kt, carry):
            base = pl.multiple_of(kt * KEY_TILE, KEY_TILE)
            v_t = kv_ref[pl.ds(base, KEY_TILE), :][:, HEAD_DIM:]
            vt_ref[kt] = jnp.transpose(v_t.astype(f32)).astype(bf16)
            return carry

        lax.fori_loop(0, n_qblk * Q_BLOCK // KEY_TILE, vt_tile, 0)

    def rope_rows(x, cos_t, sin_t):
        half = x.shape[0] // 2
        x1, x2 = x[:half], x[half:]
        return jnp.concatenate([x1 * cos_t - x2 * sin_t, x1 * sin_t + x2 * cos_t], axis=0)

    cos_i, sin_i = cos_i_ref[...], sin_i_ref[...]
    for c in range(qi_ref.shape[1] // LANES):
        t = jnp.transpose(qi_ref[:, c * LANES:(c + 1) * LANES])
        for j in range(LANES // IDX_DIM):
            h = c * (LANES // IDX_DIM) + j
            qit_ref[:, h * Q_BLOCK:(h + 1) * Q_BLOCK] = rope_rows(
                t[j * IDX_DIM:(j + 1) * IDX_DIM], cos_i, sin_i).astype(bf16)
    cos_a, sin_a, qgain = cos_a_ref[...], sin_a_ref[...], qgain_ref[...]
    for h in range(heads):
        hs = slice(h * HEAD_DIM, (h + 1) * HEAD_DIM)
        t = jnp.transpose(q_ref[:, hs])
        ms = jnp.mean(t * t, axis=0, keepdims=True)
        t = rope_rows(t * lax.rsqrt(ms + EPS) * qgain, cos_a, sin_a)
        q2t_ref[:, hs] = (t * (HEAD_DIM ** -0.5)).astype(bf16)
    w_t = jnp.transpose(smallq_ref[...])
    q_chunk = (qb * Q_BLOCK + lax.broadcasted_iota(jnp.int32, (KEY_TILE, Q_BLOCK), 1)) // CHUNK
    key_off = lax.broadcasted_iota(jnp.int32, (KEY_TILE, Q_BLOCK), 0)

    def score_group(g, carry):
        rows = SCORE_GROUP * KEY_TILE
        base = pl.multiple_of(g * rows, rows)
        ki = smallk_ref[pl.ds(base, rows), :][:, :IDX_DIM].astype(bf16)
        rel = jnp.dot(ki, qit_ref[...], preferred_element_type=f32)
        for u in range(SCORE_GROUP):
            us = slice(u * KEY_TILE, (u + 1) * KEY_TILE)
            s = jnp.zeros((KEY_TILE, Q_BLOCK), f32)
            for h in range(idx_heads):
                r = jnp.maximum(rel[us, h * Q_BLOCK:(h + 1) * Q_BLOCK], 0.0)
                s = s + w_t[IDX_DIM + h:IDX_DIM + h + 1, :] * r
            adm = (base + u * KEY_TILE + key_off) // CHUNK <= q_chunk
            keys_ref[g * SCORE_GROUP + u] = jnp.where(adm, _sortable(s), INT_MIN)
        return carry

    lax.fori_loop(0, n_grp, score_group, 0)

    def count_keys(pred):
        def count_group(g, cnt):
            for u in range(SCORE_GROUP):
                kt = g * SCORE_GROUP + u
                hit = pred(keys_ref[kt], kt * KEY_TILE + key_off).astype(jnp.int32)
                cnt = cnt + jnp.sum(hit.reshape(KEY_TILE // 8, 8, Q_BLOCK), axis=0)
            return cnt

        cnt = lax.fori_loop(0, n_grp, count_group, jnp.zeros((8, Q_BLOCK), jnp.int32))
        return jnp.sum(cnt, axis=0, keepdims=True)

    def bit_step(i, carry):
        prefix, n_ge = carry
        cand = prefix + jnp.left_shift(jnp.int32(1), 31 - i)
        total = count_keys(lambda kk, idx: kk >= cand)
        take = total >= top
        return jnp.where(take, cand, prefix), jnp.where(take, total, n_ge)

    thr, n_ge = lax.fori_loop(
        0, 32, bit_step,
        (jnp.full((1, Q_BLOCK), INT_MIN, jnp.int32), jnp.zeros((1, Q_BLOCK), jnp.int32)))

    def tie_search(_):
        def idx_step(i, bound):
            cand = bound + jnp.left_shift(jnp.int32(1), idx_bits - 1 - i)
            total = count_keys(lambda kk, idx: (kk > thr) | ((kk == thr) & (idx < cand)))
            return jnp.where(total < top, cand, bound)

        return lax.fori_loop(0, idx_bits, idx_step, jnp.zeros((1, Q_BLOCK), jnp.int32))

    idx_bits = int(keys_ref.shape[0] * KEY_TILE).bit_length()
    no_ties = lambda _: jnp.full((1, Q_BLOCK), keys_ref.shape[0] * KEY_TILE, jnp.int32)
    last_idx = lax.cond(jnp.max(n_ge) > top, tie_search, no_ties, 0)

    m_ref[...] = jnp.full(m_ref.shape, NEG_BIG, f32)
    l_ref[...] = jnp.zeros(l_ref.shape, f32)
    acc_ref[...] = jnp.zeros(acc_ref.shape, f32)

    def attn_group(g, carry):
        rows = ATTN_GROUP * KEY_TILE
        base = pl.multiple_of(g * rows, rows)
        k_t = kv_ref[pl.ds(base, rows), :][:, :HEAD_DIM]
        logits = jnp.dot(k_t, q2t_ref[...], preferred_element_type=f32)
        sel = []
        for u in range(ATTN_GROUP):
            kk = keys_ref[g * ATTN_GROUP + u]
            idx = base + u * KEY_TILE + key_off
            sel.append(((kk > thr) | ((kk == thr) & (idx <= last_idx))) & (kk != INT_MIN))
        sel = jnp.concatenate(sel, axis=0)
        probs, alphas = [], []
        for h in range(heads):
            hs = slice(h * Q_BLOCK, (h + 1) * Q_BLOCK)
            lg = jnp.where(sel, logits[:, hs], NEG_BIG)
            m_old = m_ref[:, hs]
            m_new = jnp.maximum(m_old, jnp.max(lg, axis=0, keepdims=True))
            alpha = jnp.exp(m_old - m_new)
            p = jnp.exp(lg - m_new)
            l_ref[:, hs] = alpha * l_ref[:, hs] + jnp.sum(p, axis=0, keepdims=True)
            m_ref[:, hs] = m_new
            probs.append(p.astype(bf16))
            alphas.append(alpha)
        v_t = jnp.concatenate([vt_ref[g * ATTN_GROUP + u] for u in range(ATTN_GROUP)], axis=1)
        pv = jnp.dot(v_t, jnp.concatenate(probs, axis=1),
                     preferred_element_type=f32)
        acc_ref[...] = acc_ref[...] * jnp.concatenate(alphas, axis=1) + pv
        return carry

    lax.fori_loop(0, n_grp_attn, attn_group, 0)

    out = acc_ref[...] / l_ref[...]
    for h in range(heads):
        hs = slice(h * HEAD_DIM, (h + 1) * HEAD_DIM)
        o_ref[:, hs] = jnp.transpose(out[:, hs]).astype(o_ref.dtype)


def dsa_attention(q, kv, qi, small, q_gain, rope_a, rope_i, batch, seq):
    n, aw = q.shape
    n_qblk = seq // Q_BLOCK
    n_kt = seq // KEY_TILE
    assert n_kt % SCORE_GROUP == 0 and SCORE_GROUP % ATTN_GROUP == 0
    heads = aw // HEAD_DIM
    iw = qi.shape[1]
    top = min(TOPK_MAX, seq // 4)
    q_gain_t = jnp.broadcast_to(q_gain.reshape(HEAD_DIM, 1), (HEAD_DIM, Q_BLOCK))
    per_qblk = lambda rows: pl.BlockSpec((rows, Q_BLOCK), lambda g: (0, g % n_qblk))
    return pl.pallas_call(
        functools.partial(_dsa_kernel, n_qblk=n_qblk, top=top),
        grid=(batch * n_qblk,),
        in_specs=[pl.BlockSpec((Q_BLOCK, aw), lambda g: (g, 0)),
                  pl.BlockSpec((seq, 2 * HEAD_DIM), lambda g: (g // n_qblk, 0)),
                  pl.BlockSpec((Q_BLOCK, iw), lambda g: (g, 0)),
                  pl.BlockSpec((seq, LANES), lambda g: (g // n_qblk, 0)),
                  pl.BlockSpec((Q_BLOCK, LANES), lambda g: (g, 0)),
                  pl.BlockSpec((HEAD_DIM, Q_BLOCK), lambda g: (0, 0)),
                  per_qblk(HEAD_DIM // 2), per_qblk(HEAD_DIM // 2),
                  per_qblk(IDX_DIM // 2), per_qblk(IDX_DIM // 2)],
        out_specs=pl.BlockSpec((Q_BLOCK, aw), lambda g: (g, 0)),
        out_shape=jax.ShapeDtypeStruct((n, aw), bf16),
        scratch_shapes=[pltpu.VMEM((IDX_DIM, (iw // IDX_DIM) * Q_BLOCK), bf16),
                        pltpu.VMEM((HEAD_DIM, heads * Q_BLOCK), bf16),
                        pltpu.VMEM((n_kt, HEAD_DIM, KEY_TILE), bf16),
                        pltpu.VMEM((n_kt, KEY_TILE, Q_BLOCK), jnp.int32),
                        pltpu.VMEM((1, heads * Q_BLOCK), f32),
                        pltpu.VMEM((1, heads * Q_BLOCK), f32),
                        pltpu.VMEM((HEAD_DIM, heads * Q_BLOCK), f32)],
        compiler_params=_params(("arbitrary",)),
        name="dsa_attention",
    )(q, kv, qi, small, small, q_gain_t, *rope_a, *rope_i)


def _merge_kernel(ya_ref, yb_ref, wa_ref, wb_ref, ga_ref, gb_ref, o_ref):
    a = jnp.dot(ya_ref[...], wa_ref[...], preferred_element_type=f32)
    b = jnp.dot(yb_ref[...], wb_ref[...], preferred_element_type=f32)
    o_ref[...] = (ga_ref[...].astype(f32) * a + gb_ref[...].astype(f32) * b).astype(o_ref.dtype)


def gated_merge(ya, yb, wa, wb, gates, tm=1024, tn=512):
    n, ka = ya.shape
    kb = yb.shape[1]
    d = wa.shape[1]
    tm, tn = _tile(n, tm), _tile(d, tn)
    nj = d // tn
    assert d % tn == 0
    return pl.pallas_call(
        _merge_kernel,
        grid=(pl.cdiv(n, tm), nj),
        in_specs=[pl.BlockSpec((tm, ka), lambda i, j: (i, 0)),
                  pl.BlockSpec((tm, kb), lambda i, j: (i, 0)),
                  pl.BlockSpec((ka, tn), lambda i, j: (0, j)),
                  pl.BlockSpec((kb, tn), lambda i, j: (0, j)),
                  pl.BlockSpec((tm, tn), lambda i, j: (i, j)),
                  pl.BlockSpec((tm, tn), lambda i, j: (i, j + nj))],
        out_specs=pl.BlockSpec((tm, tn), lambda i, j: (i, j)),
        out_shape=jax.ShapeDtypeStruct((n, d), bf16),
        compiler_params=_params(("parallel", "arbitrary")),
        name="gated_merge",
    )(ya, yb, wa, wb, gates, gates)


def _rope_tables(seq, dim, lanes):
    inv = ROPE_THETA ** (-jnp.arange(0, dim, 2, dtype=f32) / dim)
    ang = jnp.arange(seq, dtype=f32)[:, None] * inv[None, :]
    cos, sin = jnp.cos(ang), jnp.sin(ang)
    reps = lanes // dim
    cos_t = jnp.tile(jnp.concatenate([cos, cos], axis=-1), (1, reps))
    sin_t = jnp.tile(jnp.concatenate([-sin, sin], axis=-1), (1, reps))
    return cos_t, sin_t


def _rope_tables_t(seq, dim):
    inv = ROPE_THETA ** (-jnp.arange(0, dim, 2, dtype=f32) / dim)
    ang = inv[:, None] * jnp.arange(seq, dtype=f32)[None, :]
    return jnp.cos(ang), jnp.sin(ang)


def _fold_gain(gain, w):
    return (gain[:, None] * w).astype(bf16)


def _project(h16, h_ss, w16_t, col_off, n_cols, epilogue, out_dtype, tn, pad_to=None, **kw):
    tn = min(tn, n_cols if pad_to is None else pad_to)
    if pad_to is None and col_off % tn == 0 and n_cols % tn == 0:
        return matmul(h16, w16_t, epilogue, out_dtype, tn=tn, n_cols=n_cols, col_off=col_off,
                      row_sumsq=h_ss, w_rows=True, **kw)
    w = w16_t[col_off:col_off + n_cols]
    if pad_to is not None:
        w = jnp.pad(w, ((0, pad_to - n_cols), (0, 0)))
    return matmul(h16, w, epilogue, out_dtype, tn=tn, row_sumsq=h_ss, w_rows=True, **kw)


def kernel(x, ffn1_norm, ffn1_w1, ffn1_w3, ffn1_w2, mix_norm, w_in, gate_bias,
           gmlp_v_norm, gmlp_ws, gmlp_bs, q_norm, k_norm, idx_k_norm,
           w_br_a, w_br_b, w_out, ffn2_norm, ffn2_w1, ffn2_w3, ffn2_w2):
    batch, seq, d = x.shape
    depth = ffn1_norm.shape[0]
    n = batch * seq
    gw = gmlp_v_norm.shape[1]
    aw = w_br_b.shape[1]
    idx_heads = d // 128
    iw = idx_heads * IDX_DIM
    d_ff = ffn1_w1.shape[2]
    nk_down = 2 if (d_ff % (2 * LANES) == 0 and d_ff > 4096) else 1
    c_uv, c_q, c_kv, c_qi = 2 * gw, aw, 2 * HEAD_DIM, iw
    o_q = c_uv
    o_kv = o_q + c_q
    o_qi = o_kv + c_kv
    o_small = o_qi + c_qi
    o_gate = o_small + IDX_DIM + idx_heads
    assert o_gate + 2 * d == w_in.shape[2]
    assert IDX_DIM + idx_heads <= LANES
    idx_scale = (idx_heads ** -0.5) * (IDX_DIM ** -0.5)

    cos_a, sin_a = _rope_tables(seq, HEAD_DIM, LANES)
    cos_i, sin_i = _rope_tables(seq, IDX_DIM, LANES)
    rope_a_t = _rope_tables_t(seq, HEAD_DIM)
    rope_i_t = _rope_tables_t(seq, IDX_DIM)
    tm = _tile(seq, 1024)
    assert seq % tm == 0 and seq % Q_BLOCK == 0

    h = x.reshape(n, d)
    h16, h_ss = cast_sumsq(h)
    for l in range(depth):
        w_in_t = jnp.transpose(w_in[l])
        g, (w2_16, w_in16_t, wa16, wb16, wo16) = swiglu_up(
            h16, h_ss, _fold_gain(ffn1_norm[l], ffn1_w1[l]), _fold_gain(ffn1_norm[l], ffn1_w3[l]),
            sides=[(ffn1_w2[l], None), (w_in_t, mix_norm[l].reshape(1, d)),
                   (w_br_a[l], None), (w_br_b[l], None), (w_out[l], None)])
        h, h16, h_ss = matmul(g, w2_16, _ep_half_residual, f32, extras=[(h, "tile")],
                              nk=nk_down, emit_norm=True, name="ffn_down")

        ffn2_gain = ffn2_norm[l].reshape(d, 1)
        uv = _project(h16, h_ss, w_in16_t, 0, c_uv, _ep_gelu, bf16, WIDE_TN, tm=tm,
                      name="proj_uv")
        q = _project(h16, h_ss, w_in16_t, o_q, c_q, _ep_identity, f32, WIDE_TN, tm=tm,
                     name="proj_q")
        kv = _project(h16, h_ss, w_in16_t, o_kv, c_kv, _ep_kv, bf16, c_kv, tm=tm,
                      extras=[(k_norm[l].reshape(1, HEAD_DIM), "const"),
                              (cos_a, "pos"), (sin_a, "pos")], name="proj_kv")
        qi = _project(h16, h_ss, w_in16_t, o_qi, c_qi, _ep_identity, f32, WIDE_TN, tm=tm,
                      name="proj_qi")
        k_gain = jnp.tile(idx_k_norm[l].reshape(1, IDX_DIM), (1, LANES // IDX_DIM))
        small = _project(h16, h_ss, w_in16_t, o_small, o_gate - o_small,
                         functools.partial(_ep_small, idx_scale=idx_scale), f32, LANES,
                         pad_to=LANES, tm=tm,
                         extras=[(k_gain, "const"), (cos_i, "pos"), (sin_i, "pos")],
                         name="proj_small")
        gates, w1b_16, w3b_16 = _project(
            h16, h_ss, w_in16_t, o_gate, 2 * d, _ep_gate, bf16, WIDE_TN, tm=tm,
            extras=[(gate_bias[l].reshape(1, 2 * d), "row")],
            sides=[(ffn2_w1[l], ffn2_gain), (ffn2_w3[l], ffn2_gain)], name="proj_gate")

        ya = gmlp_gate(uv, gmlp_v_norm[l], gmlp_ws[l], gmlp_bs[l].T)
        yb = dsa_attention(q, kv, qi, small, q_norm[l], rope_a_t, rope_i_t, batch, seq)
        m = gated_merge(ya, yb, wa16, wb16, gates, tm=tm, tn=WIDE_TN)
        h, h16, h_ss = matmul(m, wo16, _ep_residual, f32, extras=[(h, "tile")],
                              tm=tm, emit_norm=True, name="out_proj")

        g, (w2b_16,) = swiglu_up(h16, h_ss, w1b_16, w3b_16, sides=[(ffn2_w2[l], None)])
        last = l == depth - 1
        out = matmul(g, w2b_16, _ep_half_residual, f32, extras=[(h, "tile")], nk=nk_down,
                     emit_norm=not last, name="ffn_down")
        h, h16, h_ss = (out, None, None) if last else out
    return h.reshape(batch, seq, d)
```

```python
import functools
import math

import jax
import jax.numpy as jnp
from jax import lax
from jax.experimental import pallas as pl
from jax.experimental.pallas import tpu as pltpu

CHUNK = 64
GROUP_DIM = 128
MIX_BLOCK = 128
HEAD_DIM = 128
IDX_DIM = 64
Q_BLOCK = 256
KEY_TILE = 128
SCORE_GROUP = 2
ATTN_GROUP = 2
DENOM_ROWS = 16
LOG2_E = math.log2(math.e)
TOPK_MAX = 256
ROPE_THETA = 10000.0
EPS = 1e-6
LANES = 128
BF16_SUBLANES = 16
NEG_BIG = -1e30
INT_MIN = -(2 ** 31)
VMEM_LIMIT_BYTES = 60000 * 1024
WIDE_TN = 1024

f32 = jnp.float32
bf16 = jnp.bfloat16


def _params(sem):
    return pltpu.CompilerParams(dimension_semantics=sem,
                                vmem_limit_bytes=VMEM_LIMIT_BYTES)


def _tile(dim, want):
    return dim if dim <= want else want


def _gelu(x):
    return 0.5 * x * (1.0 + lax.erf(x * (2.0 ** -0.5)))


def _rope_pairs(x, cos_t, sin_t, half):
    width = x.shape[-1]
    lane = lax.broadcasted_iota(jnp.int32, x.shape, x.ndim - 1)
    fwd = pltpu.roll(x, width - half, axis=x.ndim - 1)
    bwd = pltpu.roll(x, half, axis=x.ndim - 1)
    partner = jnp.where((lane % (2 * half)) < half, fwd, bwd)
    return x * cos_t + partner * sin_t


def _row_rsqrt(sumsq, width):
    return lax.rsqrt(sumsq * (1.0 / width) + EPS)


def _cast_sumsq_kernel(x_ref, o_ref, ss_ref):
    x = x_ref[...]
    o_ref[...] = x.astype(o_ref.dtype)
    ss_ref[...] = jnp.sum(x * x, axis=-1, keepdims=True)


def cast_sumsq(x, tm=256):
    n, d = x.shape
    tm = _tile(n, tm)
    return pl.pallas_call(
        _cast_sumsq_kernel,
        grid=(pl.cdiv(n, tm),),
        in_specs=[pl.BlockSpec((tm, d), lambda i: (i, 0))],
        out_specs=[pl.BlockSpec((tm, d), lambda i: (i, 0)),
                   pl.BlockSpec((tm, 1), lambda i: (i, 0))],
        out_shape=[jax.ShapeDtypeStruct((n, d), bf16),
                   jax.ShapeDtypeStruct((n, 1), f32)],
        compiler_params=_params(("parallel",)),
        name="cast_sumsq",
    )(x)


def _side_plan(sides, steps, flat_step):
    in_specs, in_arrays, out_specs, out_shapes, scaled = [], [], [], [], []
    for src, scale in sides:
        rows, cols = src.shape
        tr = BF16_SUBLANES * (-(-rows // (BF16_SUBLANES * steps)))
        n_tiles = -(-rows // tr)
        assert n_tiles <= steps
        idx = lambda *g, n_tiles=n_tiles: (jnp.minimum(flat_step(*g), n_tiles - 1), 0)
        in_specs.append(pl.BlockSpec((tr, cols), idx))
        in_arrays.append(src)
        if scale is not None:
            if scale.shape == (1, cols):
                in_specs.append(pl.BlockSpec((1, cols), lambda *g: (0, 0)))
            else:
                assert scale.shape == (rows, 1)
                in_specs.append(pl.BlockSpec((tr, 1), idx))
            in_arrays.append(scale)
        scaled.append(scale is not None)
        out_specs.append(pl.BlockSpec((tr, cols), idx))
        out_shapes.append(jax.ShapeDtypeStruct((rows, cols), bf16))
    return in_specs, in_arrays, out_specs, out_shapes, tuple(scaled)


def _take_side_inputs(it, scaled):
    return [(next(it), next(it) if s else None) for s in scaled]


def _run_sides(side_in, side_out):
    for (src_ref, scale_ref), dst_ref in zip(side_in, side_out):
        v = src_ref[...]
        if scale_ref is not None:
            v = v * scale_ref[...]
        dst_ref[...] = v.astype(dst_ref.dtype)


def _swiglu_up_kernel(*refs, scaled):
    it = iter(refs)
    x_ref, ss_ref, w1_ref, w3_ref = next(it), next(it), next(it), next(it)
    side_in = _take_side_inputs(it, scaled)
    o_ref = next(it)
    side_out = [next(it) for _ in scaled]
    x = x_ref[...]
    r = _row_rsqrt(ss_ref[...], x.shape[1])
    a = jnp.dot(x, w1_ref[...], preferred_element_type=f32) * r
    b = jnp.dot(x, w3_ref[...], preferred_element_type=f32) * r
    o_ref[...] = (a * jax.nn.sigmoid(a) * b).astype(o_ref.dtype)
    _run_sides(side_in, side_out)


def swiglu_up(x16, sumsq, w1, w3, sides=(), tm=1024, tn=512):
    n, d = x16.shape
    f = w1.shape[1]
    tm, tn = _tile(n, tm), _tile(f, tn)
    ni, nj = pl.cdiv(n, tm), pl.cdiv(f, tn)
    s_in, s_arr, s_out, s_shape, scaled = _side_plan(sides, ni * nj, lambda i, j: i * nj + j)
    outs = pl.pallas_call(
        functools.partial(_swiglu_up_kernel, scaled=scaled),
        grid=(ni, nj),
        in_specs=[pl.BlockSpec((tm, d), lambda i, j: (i, 0)),
                  pl.BlockSpec((tm, 1), lambda i, j: (i, 0)),
                  pl.BlockSpec((d, tn), lambda i, j: (0, j)),
                  pl.BlockSpec((d, tn), lambda i, j: (0, j))] + s_in,
        out_specs=[pl.BlockSpec((tm, tn), lambda i, j: (i, j))] + s_out,
        out_shape=[jax.ShapeDtypeStruct((n, f), bf16)] + s_shape,
        compiler_params=_params(("arbitrary", "arbitrary")),
        name="swiglu_up",
    )(x16, sumsq, w1, w3, *s_arr)
    return outs[0], outs[1:]


def _mm_kernel(*refs, nk, kdim, n_extra, epilogue, row_norm, emit_norm, scaled, w_rows):
    it = iter(refs)
    a_ref, w_ref = next(it), next(it)
    ss_in_ref = next(it) if row_norm else None
    extra = [next(it) for _ in range(n_extra)]
    side_in = _take_side_inputs(it, scaled)
    o_ref = next(it)
    o16_ref, ss_out_ref = (next(it), next(it)) if emit_norm else (None, None)
    side_out = [next(it) for _ in scaled]
    acc_ref = next(it) if nk > 1 else None
    j, k = pl.program_id(1), pl.program_id(2)

    if nk > 1:
        @pl.when(k == 0)
        def _():
            acc_ref[...] = jnp.zeros(acc_ref.shape, f32)

    if emit_norm:
        @pl.when((j == 0) & (k == 0))
        def _():
            ss_out_ref[...] = jnp.zeros(ss_out_ref.shape, f32)

    dims = (((1,), (1,)), ((), ())) if w_rows else (((1,), (0,)), ((), ()))
    acc = lax.dot_general(a_ref[...], w_ref[...], dims, preferred_element_type=f32)
    if nk > 1:
        acc = acc_ref[...] + acc
        acc_ref[...] = acc
    if row_norm:
        acc = acc * _row_rsqrt(ss_in_ref[...], kdim)
    out = epilogue(acc, *[e[...] for e in extra])
    o_ref[...] = out.astype(o_ref.dtype)
    if emit_norm:
        o16_ref[...] = out.astype(o16_ref.dtype)
        part_ss = jnp.sum(out * out, axis=-1, keepdims=True)
        ss_out_ref[...] += jnp.where(k == nk - 1, part_ss, 0.0)

    _run_sides(side_in, side_out)


def matmul(a, w, epilogue, out_dtype, extras=(), tm=1024, tn=512, nk=1, name="mm",
           n_cols=None, col_off=0, row_sumsq=None, emit_norm=False, sides=(), w_rows=False):
    m, kdim = a.shape
    n = w.shape[0 if w_rows else 1] if n_cols is None else n_cols
    tm, tn = _tile(m, tm), _tile(n, tn)
    assert kdim % nk == 0 and col_off % tn == 0
    tk = kdim // nk
    assert nk == 1 or tk % LANES == 0
    ni, nj, joff = pl.cdiv(m, tm), pl.cdiv(n, tn), col_off // tn
    in_specs = [pl.BlockSpec((tm, tk), lambda i, j, k: (i, k)),
                pl.BlockSpec((tn, tk), lambda i, j, k: (j + joff, k)) if w_rows else
                pl.BlockSpec((tk, tn), lambda i, j, k: (k, j + joff))]
    arrays = []
    if row_sumsq is not None:
        in_specs.append(pl.BlockSpec((tm, 1), lambda i, j, k: (i, 0)))
        arrays.append(row_sumsq)
    n_extra = len(extras)
    for arr, kind in extras:
        arrays.append(arr)
        if kind == "tile":
            in_specs.append(pl.BlockSpec((tm, tn), lambda i, j, k: (i, j)))
        elif kind == "row":
            in_specs.append(pl.BlockSpec((1, tn), lambda i, j, k: (0, j)))
        elif kind == "const":
            in_specs.append(pl.BlockSpec(arr.shape, lambda i, j, k: (0, 0)))
        elif kind == "pos":
            assert arr.shape[0] % tm == 0
            nper = arr.shape[0] // tm
            in_specs.append(pl.BlockSpec((tm, arr.shape[1]),
                                         lambda i, j, k, nper=nper: (i % nper, 0)))
        else:
            raise ValueError(kind)
    s_in, s_arr, s_out, s_shape, scaled = _side_plan(
        sides, ni * nj * nk, lambda i, j, k: (i * nj + j) * nk + k)
    out_specs = [pl.BlockSpec((tm, tn), lambda i, j, k: (i, j))]
    out_shape = [jax.ShapeDtypeStruct((m, n), out_dtype)]
    if emit_norm:
        out_specs += [pl.BlockSpec((tm, tn), lambda i, j, k: (i, j)),
                      pl.BlockSpec((tm, 1), lambda i, j, k: (i, 0))]
        out_shape += [jax.ShapeDtypeStruct((m, n), bf16), jax.ShapeDtypeStruct((m, 1), f32)]
    scratch = [pltpu.VMEM((tm, tn), f32)] if nk > 1 else []
    outs = pl.pallas_call(
        functools.partial(_mm_kernel, nk=nk, kdim=kdim, n_extra=n_extra, epilogue=epilogue,
                          row_norm=row_sumsq is not None, emit_norm=emit_norm, scaled=scaled,
                          w_rows=w_rows),
        grid=(ni, nj, nk),
        in_specs=in_specs + s_in,
        out_specs=out_specs + s_out,
        out_shape=out_shape + s_shape,
        scratch_shapes=scratch,
        compiler_params=_params(("arbitrary", "arbitrary", "arbitrary")),
        name=name,
    )(a, w, *arrays, *s_arr)
    return outs[0] if len(outs) == 1 else tuple(outs)


def _ep_half_residual(acc, res):
    return res + 0.5 * acc


def _ep_residual(acc, res):
    return res + acc


def _ep_gelu(acc):
    return _gelu(acc)


def _ep_gate(acc, bias):
    return jax.nn.sigmoid(acc + bias)


def _ep_identity(acc):
    return acc


def _ep_kv(acc, gain, cos_t, sin_t):
    k = acc[:, :HEAD_DIM]
    ms = jnp.mean(k * k, axis=-1, keepdims=True)
    k = _rope_pairs(k * lax.rsqrt(ms + EPS) * gain, cos_t, sin_t, HEAD_DIM // 2)
    return jnp.concatenate([k, acc[:, HEAD_DIM:]], axis=-1)


def _ep_small(acc, gain, cos_t, sin_t, *, idx_scale):
    lane = lax.broadcasted_iota(jnp.int32, acc.shape, 1)
    is_k = lane < IDX_DIM
    ms = jnp.sum(jnp.where(is_k, acc * acc, 0.0), axis=-1, keepdims=True) * (1.0 / IDX_DIM)
    k = _rope_pairs(acc * lax.rsqrt(ms + EPS) * gain, cos_t, sin_t, IDX_DIM // 2)
    return jnp.where(is_k, k, acc * idx_scale)


def _gmlp_kernel(u_ref, v_ref, gain_ref, ws_ref, bs_ref, o_ref):
    groups = ws_ref.shape[0]
    v = v_ref[...].astype(f32)
    ms = jnp.mean(v * v, axis=-1, keepdims=True)
    vn = (v * lax.rsqrt(ms + EPS) * gain_ref[...]).astype(bf16)
    row_chunk = lax.broadcasted_iota(jnp.int32, (MIX_BLOCK, MIX_BLOCK), 0) // CHUNK
    col_chunk = lax.broadcasted_iota(jnp.int32, (MIX_BLOCK, MIX_BLOCK), 1) // CHUNK
    causal = col_chunk <= row_chunk
    bs = bs_ref[...]
    for g in range(groups):
        sl = slice(g * GROUP_DIM, (g + 1) * GROUP_DIM)
        w = jnp.where(causal, ws_ref[g], 0.0).astype(bf16)
        mixed = jnp.dot(w, vn[:, sl], preferred_element_type=f32) + bs[:, g:g + 1]
        o_ref[:, sl] = (u_ref[:, sl].astype(f32) * mixed).astype(o_ref.dtype)


def gmlp_gate(uv, gain, ws, bs_t):
    n = uv.shape[0]
    gw = uv.shape[1] // 2
    groups = gw // GROUP_DIM
    return pl.pallas_call(
        _gmlp_kernel,
        grid=(n // MIX_BLOCK,),
        in_specs=[pl.BlockSpec((MIX_BLOCK, gw), lambda i: (i, 0)),
                  pl.BlockSpec((MIX_BLOCK, gw), lambda i: (i, 1)),
                  pl.BlockSpec((1, gw), lambda i: (0, 0)),
                  pl.BlockSpec((groups, MIX_BLOCK, MIX_BLOCK), lambda i: (0, 0, 0)),
                  pl.BlockSpec((MIX_BLOCK, groups), lambda i: (0, 0))],
        out_specs=pl.BlockSpec((MIX_BLOCK, gw), lambda i: (i, 0)),
        out_shape=jax.ShapeDtypeStruct((n, gw), bf16),
        compiler_params=_params(("parallel",)),
        name="gmlp_gate",
    )(uv, uv, gain.reshape(1, gw), ws, bs_t)


def _sortable(x):
    bits = lax.bitcast_convert_type(x, jnp.int32)
    return bits ^ ((bits >> 31) & jnp.int32(0x7FFFFFFF))


def _dsa_kernel(q_ref, kv_ref, qi_ref, smallk_ref, smallq_ref, qgain_ref,
                cos_a_ref, sin_a_ref, cos_i_ref, sin_i_ref, o_ref,
                qit_ref, q2t_ref, vt_ref, keys_ref, m_ref, acc_ref,
                *, n_qblk, top):
    qb = pl.program_id(0) % n_qblk
    n_kt = (qb + 1) * (Q_BLOCK // KEY_TILE)
    n_grp = (n_kt + SCORE_GROUP - 1) // SCORE_GROUP
    n_grp_attn = (n_kt + ATTN_GROUP - 1) // ATTN_GROUP
    heads = q_ref.shape[1] // HEAD_DIM
    idx_heads = qi_ref.shape[1] // IDX_DIM

    @pl.when(qb == 0)
    def _():
        def vt_tile(kt, carry):
            base = pl.multiple_of(kt * KEY_TILE, KEY_TILE)
            v_t = kv_ref[pl.ds(base, KEY_TILE), :][:, HEAD_DIM:]
            vt_ref[kt, :HEAD_DIM] = jnp.transpose(v_t.astype(f32)).astype(bf16)
            vt_ref[kt, HEAD_DIM:] = jnp.ones((DENOM_ROWS, KEY_TILE), bf16)
            return carry

        lax.fori_loop(0, n_qblk * Q_BLOCK // KEY_TILE, vt_tile, 0)

    def rope_rows(x, cos_t, sin_t):
        half = x.shape[0] // 2
        x1, x2 = x[:half], x[half:]
        return jnp.concatenate([x1 * cos_t - x2 * sin_t, x1 * sin_t + x2 * cos_t], axis=0)

    cos_i, sin_i = cos_i_ref[...], sin_i_ref[...]
    for c in range(qi_ref.shape[1] // LANES):
        t = jnp.transpose(qi_ref[:, c * LANES:(c + 1) * LANES])
        for j in range(LANES // IDX_DIM):
            h = c * (LANES // IDX_DIM) + j
            qit_ref[:, h * Q_BLOCK:(h + 1) * Q_BLOCK] = rope_rows(
                t[j * IDX_DIM:(j + 1) * IDX_DIM], cos_i, sin_i).astype(bf16)
    cos_a, sin_a, qgain = cos_a_ref[...], sin_a_ref[...], qgain_ref[...]
    for h in range(heads):
        t = jnp.transpose(q_ref[:, h * HEAD_DIM:(h + 1) * HEAD_DIM])
        ms = jnp.mean(t * t, axis=0, keepdims=True)
        t = rope_rows(t * lax.rsqrt(ms + EPS) * qgain, cos_a, sin_a)
        q2t_ref[:, h * Q_BLOCK:(h + 1) * Q_BLOCK] = (t * (HEAD_DIM ** -0.5 * LOG2_E)).astype(bf16)
    w_t = jnp.transpose(smallq_ref[...])
    q_chunk = (qb * Q_BLOCK + lax.broadcasted_iota(jnp.int32, (KEY_TILE, Q_BLOCK), 1)) // CHUNK
    key_off = lax.broadcasted_iota(jnp.int32, (KEY_TILE, Q_BLOCK), 0)

    def score_group(g, carry):
        rows = SCORE_GROUP * KEY_TILE
        base = pl.multiple_of(g * rows, rows)
        ki = smallk_ref[pl.ds(base, rows), :][:, :IDX_DIM].astype(bf16)
        rel = jnp.dot(ki, qit_ref[...], preferred_element_type=f32)
        for u in range(SCORE_GROUP):
            us = slice(u * KEY_TILE, (u + 1) * KEY_TILE)
            s = jnp.zeros((KEY_TILE, Q_BLOCK), f32)
            for h in range(idx_heads):
                r = jnp.maximum(rel[us, h * Q_BLOCK:(h + 1) * Q_BLOCK], 0.0)
                s = s + w_t[IDX_DIM + h:IDX_DIM + h + 1, :] * r
            adm = (base + u * KEY_TILE + key_off) // CHUNK <= q_chunk
            keys_ref[g * SCORE_GROUP + u] = jnp.where(adm, _sortable(s), INT_MIN)
        return carry

    lax.fori_loop(0, n_grp, score_group, 0)

    def count_keys(pred):
        def count_group(g, cnt):
            for u in range(SCORE_GROUP):
                kt = g * SCORE_GROUP + u
                hit = pred(keys_ref[kt], kt * KEY_TILE + key_off).astype(jnp.int32)
                cnt = cnt + jnp.sum(hit.reshape(KEY_TILE // 8, 8, Q_BLOCK), axis=0)
            return cnt

        cnt = lax.fori_loop(0, n_grp, count_group, jnp.zeros((8, Q_BLOCK), jnp.int32))
        return jnp.sum(cnt, axis=0, keepdims=True)

    def bit_step(i, carry):
        prefix, n_ge = carry
        cand = prefix + jnp.left_shift(jnp.int32(1), 31 - i)
        total = count_keys(lambda kk, idx: kk >= cand)
        take = total >= top
        return jnp.where(take, cand, prefix), jnp.where(take, total, n_ge)

    thr, n_ge = lax.fori_loop(
        0, 32, bit_step,
        (jnp.full((1, Q_BLOCK), INT_MIN, jnp.int32), jnp.zeros((1, Q_BLOCK), jnp.int32)))

    def tie_search(_):
        def idx_step(i, bound):
            cand = bound + jnp.left_shift(jnp.int32(1), idx_bits - 1 - i)
            total = count_keys(lambda kk, idx: (kk > thr) | ((kk == thr) & (idx < cand)))
            return jnp.where(total < top, cand, bound)

        return lax.fori_loop(0, idx_bits, idx_step, jnp.zeros((1, Q_BLOCK), jnp.int32))

    idx_bits = int(keys_ref.shape[0] * KEY_TILE).bit_length()
    no_ties = lambda _: jnp.full((1, Q_BLOCK), keys_ref.shape[0] * KEY_TILE, jnp.int32)
    last_idx = lax.cond(jnp.max(n_ge) > top, tie_search, no_ties, 0)

    m_ref[...] = jnp.full(m_ref.shape, NEG_BIG, f32)
    acc_ref[...] = jnp.zeros(acc_ref.shape, f32)

    def attn_group(g, carry):
        rows = ATTN_GROUP * KEY_TILE
        base = pl.multiple_of(g * rows, rows)
        k_t = kv_ref[pl.ds(base, rows), :][:, :HEAD_DIM]
        sel = []
        for u in range(ATTN_GROUP):
            kk = keys_ref[g * ATTN_GROUP + u]
            idx = base + u * KEY_TILE + key_off
            sel.append(((kk > thr) | ((kk == thr) & (idx <= last_idx))) & (kk != INT_MIN))
        sel = jnp.concatenate(sel, axis=0)
        v_t = jnp.concatenate([vt_ref[g * ATTN_GROUP + u] for u in range(ATTN_GROUP)], axis=1)
        logits = jnp.dot(k_t, q2t_ref[...], preferred_element_type=f32)
        probs, alphas = [], []
        for h in range(heads):
            hs = slice(h * Q_BLOCK, (h + 1) * Q_BLOCK)
            lg = jnp.where(sel, logits[:, hs], NEG_BIG)
            m_old = m_ref[:, hs]
            m_new = jnp.maximum(m_old, jnp.max(lg, axis=0, keepdims=True))
            m_ref[:, hs] = m_new
            alphas.append(jnp.exp2(m_old - m_new))
            probs.append(jnp.exp2(lg - m_new).astype(bf16))
        pv = jnp.dot(v_t, jnp.concatenate(probs, axis=1),
                     preferred_element_type=f32)
        acc_ref[...] = acc_ref[...] * jnp.concatenate(alphas, axis=1) + pv
        return carry

    lax.fori_loop(0, n_grp_attn, attn_group, 0)

    out = acc_ref[:HEAD_DIM] / acc_ref[HEAD_DIM:HEAD_DIM + 1]
    for h in range(heads):
        o_ref[:, h * HEAD_DIM:(h + 1) * HEAD_DIM] = jnp.transpose(
            out[:, h * Q_BLOCK:(h + 1) * Q_BLOCK]).astype(o_ref.dtype)


def dsa_attention(q, kv, qi, small, q_gain, rope_a, rope_i, batch, seq):
    n, aw = q.shape
    n_qblk = seq // Q_BLOCK
    n_kt = seq // KEY_TILE
    assert n_kt % SCORE_GROUP == 0 and SCORE_GROUP % ATTN_GROUP == 0
    heads = aw // HEAD_DIM
    iw = qi.shape[1]
    top = min(TOPK_MAX, seq // 4)
    q_gain_t = jnp.broadcast_to(q_gain.reshape(HEAD_DIM, 1), (HEAD_DIM, Q_BLOCK))
    per_qblk = lambda rows: pl.BlockSpec((rows, Q_BLOCK), lambda g: (0, g % n_qblk))
    return pl.pallas_call(
        functools.partial(_dsa_kernel, n_qblk=n_qblk, top=top),
        grid=(batch * n_qblk,),
        in_specs=[pl.BlockSpec((Q_BLOCK, aw), lambda g: (g, 0)),
                  pl.BlockSpec((seq, 2 * HEAD_DIM), lambda g: (g // n_qblk, 0)),
                  pl.BlockSpec((Q_BLOCK, iw), lambda g: (g, 0)),
                  pl.BlockSpec((seq, LANES), lambda g: (g // n_qblk, 0)),
                  pl.BlockSpec((Q_BLOCK, LANES), lambda g: (g, 0)),
                  pl.BlockSpec((HEAD_DIM, Q_BLOCK), lambda g: (0, 0)),
                  per_qblk(HEAD_DIM // 2), per_qblk(HEAD_DIM // 2),
                  per_qblk(IDX_DIM // 2), per_qblk(IDX_DIM // 2)],
        out_specs=pl.BlockSpec((Q_BLOCK, aw), lambda g: (g, 0)),
        out_shape=jax.ShapeDtypeStruct((n, aw), bf16),
        scratch_shapes=[pltpu.VMEM((IDX_DIM, (iw // IDX_DIM) * Q_BLOCK), bf16),
                        pltpu.VMEM((HEAD_DIM, heads * Q_BLOCK), bf16),
                        pltpu.VMEM((n_kt, HEAD_DIM + DENOM_ROWS, KEY_TILE), bf16),
                        pltpu.VMEM((n_kt, KEY_TILE, Q_BLOCK), jnp.int32),
                        pltpu.VMEM((1, heads * Q_BLOCK), f32),
                        pltpu.VMEM((HEAD_DIM + DENOM_ROWS, heads * Q_BLOCK), f32)],
        compiler_params=_params(("arbitrary",)),
        name="dsa_attention",
    )(q, kv, qi, small, small, q_gain_t, *rope_a, *rope_i)


def _merge_kernel(ya_ref, yb_ref, wa_ref, wb_ref, ga_ref, gb_ref, o_ref):
    a = jnp.dot(ya_ref[...], wa_ref[...], preferred_element_type=f32)
    b = jnp.dot(yb_ref[...], wb_ref[...], preferred_element_type=f32)
    o_ref[...] = (ga_ref[...].astype(f32) * a + gb_ref[...].astype(f32) * b).astype(o_ref.dtype)


def gated_merge(ya, yb, wa, wb, gates, tm=1024, tn=512):
    n, ka = ya.shape
    kb = yb.shape[1]
    d = wa.shape[1]
    tm, tn = _tile(n, tm), _tile(d, tn)
    nj = d // tn
    assert d % tn == 0
    return pl.pallas_call(
        _merge_kernel,
        grid=(pl.cdiv(n, tm), nj),
        in_specs=[pl.BlockSpec((tm, ka), lambda i, j: (i, 0)),
                  pl.BlockSpec((tm, kb), lambda i, j: (i, 0)),
                  pl.BlockSpec((ka, tn), lambda i, j: (0, j)),
                  pl.BlockSpec((kb, tn), lambda i, j: (0, j)),
                  pl.BlockSpec((tm, tn), lambda i, j: (i, j)),
                  pl.BlockSpec((tm, tn), lambda i, j: (i, j + nj))],
        out_specs=pl.BlockSpec((tm, tn), lambda i, j: (i, j)),
        out_shape=jax.ShapeDtypeStruct((n, d), bf16),
        compiler_params=_params(("parallel", "arbitrary")),
        name="gated_merge",
    )(ya, yb, wa, wb, gates, gates)


def _rope_tables(seq, dim, lanes):
    inv = ROPE_THETA ** (-jnp.arange(0, dim, 2, dtype=f32) / dim)
    ang = jnp.arange(seq, dtype=f32)[:, None] * inv[None, :]
    cos, sin = jnp.cos(ang), jnp.sin(ang)
    reps = lanes // dim
    cos_t = jnp.tile(jnp.concatenate([cos, cos], axis=-1), (1, reps))
    sin_t = jnp.tile(jnp.concatenate([-sin, sin], axis=-1), (1, reps))
    return cos_t, sin_t


def _rope_tables_t(seq, dim):
    inv = ROPE_THETA ** (-jnp.arange(0, dim, 2, dtype=f32) / dim)
    ang = inv[:, None] * jnp.arange(seq, dtype=f32)[None, :]
    return jnp.cos(ang), jnp.sin(ang)


def _fold_gain(gain, w):
    return (gain[:, None] * w).astype(bf16)


def _project(h16, h_ss, w16_t, col_off, n_cols, epilogue, out_dtype, tn, pad_to=None, **kw):
    tn = min(tn, n_cols if pad_to is None else pad_to)
    if pad_to is None and col_off % tn == 0 and n_cols % tn == 0:
        return matmul(h16, w16_t, epilogue, out_dtype, tn=tn, n_cols=n_cols, col_off=col_off,
                      row_sumsq=h_ss, w_rows=True, **kw)
    w = w16_t[col_off:col_off + n_cols]
    if pad_to is not None:
        w = jnp.pad(w, ((0, pad_to - n_cols), (0, 0)))
    return matmul(h16, w, epilogue, out_dtype, tn=tn, row_sumsq=h_ss, w_rows=True, **kw)


def kernel(x, ffn1_norm, ffn1_w1, ffn1_w3, ffn1_w2, mix_norm, w_in, gate_bias,
           gmlp_v_norm, gmlp_ws, gmlp_bs, q_norm, k_norm, idx_k_norm,
           w_br_a, w_br_b, w_out, ffn2_norm, ffn2_w1, ffn2_w3, ffn2_w2):
    batch, seq, d = x.shape
    depth = ffn1_norm.shape[0]
    n = batch * seq
    gw = gmlp_v_norm.shape[1]
    aw = w_br_b.shape[1]
    idx_heads = d // 128
    iw = idx_heads * IDX_DIM
    d_ff = ffn1_w1.shape[2]
    nk_down = 2 if (d_ff % (2 * LANES) == 0 and d_ff > 4096) else 1
    c_uv, c_q, c_kv, c_qi = 2 * gw, aw, 2 * HEAD_DIM, iw
    o_q = c_uv
    o_kv = o_q + c_q
    o_qi = o_kv + c_kv
    o_small = o_qi + c_qi
    o_gate = o_small + IDX_DIM + idx_heads
    assert o_gate + 2 * d == w_in.shape[2]
    assert IDX_DIM + idx_heads <= LANES
    idx_scale = (idx_heads ** -0.5) * (IDX_DIM ** -0.5)

    cos_a, sin_a = _rope_tables(seq, HEAD_DIM, LANES)
    cos_i, sin_i = _rope_tables(seq, IDX_DIM, LANES)
    rope_a_t = _rope_tables_t(seq, HEAD_DIM)
    rope_i_t = _rope_tables_t(seq, IDX_DIM)
    tm = _tile(seq, 1024)
    assert seq % tm == 0 and seq % Q_BLOCK == 0

    h = x.reshape(n, d)
    h16, h_ss = cast_sumsq(h)
    for l in range(depth):
        w_in_t = jnp.transpose(w_in[l])
        g, (w2_16, w_in16_t, wa16, wb16, wo16) = swiglu_up(
            h16, h_ss, _fold_gain(ffn1_norm[l], ffn1_w1[l]), _fold_gain(ffn1_norm[l], ffn1_w3[l]),
            sides=[(ffn1_w2[l], None), (w_in_t, mix_norm[l].reshape(1, d)),
                   (w_br_a[l], None), (w_br_b[l], None), (w_out[l], None)])
        h, h16, h_ss = matmul(g, w2_16, _ep_half_residual, f32, extras=[(h, "tile")],
                              nk=nk_down, emit_norm=True, name="ffn_down")

        ffn2_gain = ffn2_norm[l].reshape(d, 1)
        uv = _project(h16, h_ss, w_in16_t, 0, c_uv, _ep_gelu, bf16, WIDE_TN, tm=tm,
                      name="proj_uv")
        q = _project(h16, h_ss, w_in16_t, o_q, c_q, _ep_identity, f32, WIDE_TN, tm=tm,
                     name="proj_q")
        kv = _project(h16, h_ss, w_in16_t, o_kv, c_kv, _ep_kv, bf16, c_kv, tm=tm,
                      extras=[(k_norm[l].reshape(1, HEAD_DIM), "const"),
                              (cos_a, "pos"), (sin_a, "pos")], name="proj_kv")
        qi = _project(h16, h_ss, w_in16_t, o_qi, c_qi, _ep_identity, f32, WIDE_TN, tm=tm,
                      name="proj_qi")
        k_gain = jnp.tile(idx_k_norm[l].reshape(1, IDX_DIM), (1, LANES // IDX_DIM))
        small = _project(h16, h_ss, w_in16_t, o_small, o_gate - o_small,
                         functools.partial(_ep_small, idx_scale=idx_scale), f32, LANES,
                         pad_to=LANES, tm=tm,
                         extras=[(k_gain, "const"), (cos_i, "pos"), (sin_i, "pos")],
                         name="proj_small")
        gates, w1b_16, w3b_16 = _project(
            h16, h_ss, w_in16_t, o_gate, 2 * d, _ep_gate, bf16, WIDE_TN, tm=tm,
            extras=[(gate_bias[l].reshape(1, 2 * d), "row")],
            sides=[(ffn2_w1[l], ffn2_gain), (ffn2_w3[l], ffn2_gain)], name="proj_gate")

        ya = gmlp_gate(uv, gmlp_v_norm[l], gmlp_ws[l], gmlp_bs[l].T)
        yb = dsa_attention(q, kv, qi, small, q_norm[l], rope_a_t, rope_i_t, batch, seq)
        m = gated_merge(ya, yb, wa16, wb16, gates, tm=tm, tn=WIDE_TN)
        h, h16, h_ss = matmul(m, wo16, _ep_residual, f32, extras=[(h, "tile")],
                              tm=tm, emit_norm=True, name="out_proj")

        g, (w2b_16,) = swiglu_up(h16, h_ss, w1b_16, w3b_16, sides=[(ffn2_w2[l], None)])
        last = l == depth - 1
        out = matmul(g, w2b_16, _ep_half_residual, f32, extras=[(h, "tile")], nk=nk_down,
                     emit_norm=not last, name="ffn_down")
        h, h16, h_ss = (out, None, None) if last else out
    return h.reshape(batch, seq, d)
```

```python
import functools
import math

import jax
import jax.numpy as jnp
from jax import lax
from jax.experimental import pallas as pl
from jax.experimental.pallas import tpu as pltpu

CHUNK = 64
GROUP_DIM = 128
MIX_BLOCK = 128
HEAD_DIM = 128
IDX_DIM = 64
Q_BLOCK = 256
KEY_TILE = 128
SCORE_GROUP = 2
ATTN_GROUP = 2
DENOM_ROWS = 16
LOG2_E = math.log2(math.e)
TOPK_MAX = 256
ROPE_THETA = 10000.0
EPS = 1e-6
LANES = 128
BF16_SUBLANES = 16
NEG_BIG = -1e30
INT_MIN = -(2 ** 31)
VMEM_LIMIT_BYTES = 60000 * 1024
WIDE_TN = 1024

f32 = jnp.float32
bf16 = jnp.bfloat16


def _params(sem):
    return pltpu.CompilerParams(dimension_semantics=sem,
                                vmem_limit_bytes=VMEM_LIMIT_BYTES)


def _tile(dim, want):
    return dim if dim <= want else want


def _gelu(x):
    return 0.5 * x * (1.0 + lax.erf(x * (2.0 ** -0.5)))


def _rope_pairs(x, cos_t, sin_t, half):
    width = x.shape[-1]
    lane = lax.broadcasted_iota(jnp.int32, x.shape, x.ndim - 1)
    fwd = pltpu.roll(x, width - half, axis=x.ndim - 1)
    bwd = pltpu.roll(x, half, axis=x.ndim - 1)
    partner = jnp.where((lane % (2 * half)) < half, fwd, bwd)
    return x * cos_t + partner * sin_t


def _row_rsqrt(sumsq, width):
    return lax.rsqrt(sumsq * (1.0 / width) + EPS)


def _cast_sumsq_kernel(x_ref, o_ref, ss_ref):
    x = x_ref[...]
    o_ref[...] = x.astype(o_ref.dtype)
    ss_ref[...] = jnp.sum(x * x, axis=-1, keepdims=True)


def cast_sumsq(x, tm=256):
    n, d = x.shape
    tm = _tile(n, tm)
    return pl.pallas_call(
        _cast_sumsq_kernel,
        grid=(pl.cdiv(n, tm),),
        in_specs=[pl.BlockSpec((tm, d), lambda i: (i, 0))],
        out_specs=[pl.BlockSpec((tm, d), lambda i: (i, 0)),
                   pl.BlockSpec((tm, 1), lambda i: (i, 0))],
        out_shape=[jax.ShapeDtypeStruct((n, d), bf16),
                   jax.ShapeDtypeStruct((n, 1), f32)],
        compiler_params=_params(("parallel",)),
        name="cast_sumsq",
    )(x)


def _side_plan(sides, steps, flat_step):
    in_specs, in_arrays, out_specs, out_shapes, scaled = [], [], [], [], []
    for src, scale in sides:
        rows, cols = src.shape
        tr = BF16_SUBLANES * (-(-rows // (BF16_SUBLANES * steps)))
        n_tiles = -(-rows // tr)
        assert n_tiles <= steps
        idx = lambda *g, n_tiles=n_tiles: (jnp.minimum(flat_step(*g), n_tiles - 1), 0)
        in_specs.append(pl.BlockSpec((tr, cols), idx))
        in_arrays.append(src)
        if scale is not None:
            if scale.shape == (1, cols):
                in_specs.append(pl.BlockSpec((1, cols), lambda *g: (0, 0)))
            else:
                assert scale.shape == (rows, 1)
                in_specs.append(pl.BlockSpec((tr, 1), idx))
            in_arrays.append(scale)
        scaled.append(scale is not None)
        out_specs.append(pl.BlockSpec((tr, cols), idx))
        out_shapes.append(jax.ShapeDtypeStruct((rows, cols), bf16))
    return in_specs, in_arrays, out_specs, out_shapes, tuple(scaled)


def _take_side_inputs(it, scaled):
    return [(next(it), next(it) if s else None) for s in scaled]


def _run_sides(side_in, side_out):
    for (src_ref, scale_ref), dst_ref in zip(side_in, side_out):
        v = src_ref[...]
        if scale_ref is not None:
            v = v * scale_ref[...]
        dst_ref[...] = v.astype(dst_ref.dtype)


def _swiglu_up_kernel(*refs, scaled):
    it = iter(refs)
    x_ref, ss_ref, w1_ref, w3_ref = next(it), next(it), next(it), next(it)
    side_in = _take_side_inputs(it, scaled)
    o_ref = next(it)
    side_out = [next(it) for _ in scaled]
    x = x_ref[...]
    r = _row_rsqrt(ss_ref[...], x.shape[1])
    a = jnp.dot(x, w1_ref[...], preferred_element_type=f32) * r
    b = jnp.dot(x, w3_ref[...], preferred_element_type=f32) * r
    o_ref[...] = (a * jax.nn.sigmoid(a) * b).astype(o_ref.dtype)
    _run_sides(side_in, side_out)


def swiglu_up(x16, sumsq, w1, w3, sides=(), tm=1024, tn=512):
    n, d = x16.shape
    f = w1.shape[1]
    tm, tn = _tile(n, tm), _tile(f, tn)
    ni, nj = pl.cdiv(n, tm), pl.cdiv(f, tn)
    s_in, s_arr, s_out, s_shape, scaled = _side_plan(sides, ni * nj, lambda i, j: i * nj + j)
    outs = pl.pallas_call(
        functools.partial(_swiglu_up_kernel, scaled=scaled),
        grid=(ni, nj),
        in_specs=[pl.BlockSpec((tm, d), lambda i, j: (i, 0)),
                  pl.BlockSpec((tm, 1), lambda i, j: (i, 0)),
                  pl.BlockSpec((d, tn), lambda i, j: (0, j)),
                  pl.BlockSpec((d, tn), lambda i, j: (0, j))] + s_in,
        out_specs=[pl.BlockSpec((tm, tn), lambda i, j: (i, j))] + s_out,
        out_shape=[jax.ShapeDtypeStruct((n, f), bf16)] + s_shape,
        compiler_params=_params(("arbitrary", "arbitrary")),
        name="swiglu_up",
    )(x16, sumsq, w1, w3, *s_arr)
    return outs[0], outs[1:]


def _mm_kernel(*refs, nk, kdim, n_extra, epilogue, row_norm, emit_norm, scaled, w_rows):
    it = iter(refs)
    a_ref, w_ref = next(it), next(it)
    ss_in_ref = next(it) if row_norm else None
    extra = [next(it) for _ in range(n_extra)]
    side_in = _take_side_inputs(it, scaled)
    o_ref = next(it)
    o16_ref, ss_out_ref = (next(it), next(it)) if emit_norm else (None, None)
    side_out = [next(it) for _ in scaled]
    acc_ref = next(it) if nk > 1 else None
    j, k = pl.program_id(1), pl.program_id(2)

    if nk > 1:
        @pl.when(k == 0)
        def _():
            acc_ref[...] = jnp.zeros(acc_ref.shape, f32)

    if emit_norm:
        @pl.when((j == 0) & (k == 0))
        def _():
            ss_out_ref[...] = jnp.zeros(ss_out_ref.shape, f32)

    dims = (((1,), (1,)), ((), ())) if w_rows else (((1,), (0,)), ((), ()))
    acc = lax.dot_general(a_ref[...], w_ref[...], dims, preferred_element_type=f32)
    if nk > 1:
        acc = acc_ref[...] + acc
        acc_ref[...] = acc
    if row_norm:
        acc = acc * _row_rsqrt(ss_in_ref[...], kdim)
    out = epilogue(acc, *[e[...] for e in extra])
    o_ref[...] = out.astype(o_ref.dtype)
    if emit_norm:
        o16_ref[...] = out.astype(o16_ref.dtype)
        part_ss = jnp.sum(out * out, axis=-1, keepdims=True)
        ss_out_ref[...] += jnp.where(k == nk - 1, part_ss, 0.0)

    _run_sides(side_in, side_out)


def matmul(a, w, epilogue, out_dtype, extras=(), tm=1024, tn=512, nk=1, name="mm",
           n_cols=None, col_off=0, row_sumsq=None, emit_norm=False, sides=(), w_rows=False):
    m, kdim = a.shape
    n = w.shape[0 if w_rows else 1] if n_cols is None else n_cols
    tm, tn = _tile(m, tm), _tile(n, tn)
    assert kdim % nk == 0 and col_off % tn == 0
    tk = kdim // nk
    assert nk == 1 or tk % LANES == 0
    ni, nj, joff = pl.cdiv(m, tm), pl.cdiv(n, tn), col_off // tn
    in_specs = [pl.BlockSpec((tm, tk), lambda i, j, k: (i, k)),
                pl.BlockSpec((tn, tk), lambda i, j, k: (j + joff, k)) if w_rows else
                pl.BlockSpec((tk, tn), lambda i, j, k: (k, j + joff))]
    arrays = []
    if row_sumsq is not None:
        in_specs.append(pl.BlockSpec((tm, 1), lambda i, j, k: (i, 0)))
        arrays.append(row_sumsq)
    n_extra = len(extras)
    for arr, kind in extras:
        arrays.append(arr)
        if kind == "tile":
            in_specs.append(pl.BlockSpec((tm, tn), lambda i, j, k: (i, j)))
        elif kind == "row":
            in_specs.append(pl.BlockSpec((1, tn), lambda i, j, k: (0, j)))
        elif kind == "const":
            in_specs.append(pl.BlockSpec(arr.shape, lambda i, j, k: (0, 0)))
        elif kind == "pos":
            assert arr.shape[0] % tm == 0
            nper = arr.shape[0] // tm
            in_specs.append(pl.BlockSpec((tm, arr.shape[1]),
                                         lambda i, j, k, nper=nper: (i % nper, 0)))
        else:
            raise ValueError(kind)
    s_in, s_arr, s_out, s_shape, scaled = _side_plan(
        sides, ni * nj * nk, lambda i, j, k: (i * nj + j) * nk + k)
    out_specs = [pl.BlockSpec((tm, tn), lambda i, j, k: (i, j))]
    out_shape = [jax.ShapeDtypeStruct((m, n), out_dtype)]
    if emit_norm:
        out_specs += [pl.BlockSpec((tm, tn), lambda i, j, k: (i, j)),
                      pl.BlockSpec((tm, 1), lambda i, j, k: (i, 0))]
        out_shape += [jax.ShapeDtypeStruct((m, n), bf16), jax.ShapeDtypeStruct((m, 1), f32)]
    scratch = [pltpu.VMEM((tm, tn), f32)] if nk > 1 else []
    outs = pl.pallas_call(
        functools.partial(_mm_kernel, nk=nk, kdim=kdim, n_extra=n_extra, epilogue=epilogue,
                          row_norm=row_sumsq is not None, emit_norm=emit_norm, scaled=scaled,
                          w_rows=w_rows),
        grid=(ni, nj, nk),
        in_specs=in_specs + s_in,
        out_specs=out_specs + s_out,
        out_shape=out_shape + s_shape,
        scratch_shapes=scratch,
        compiler_params=_params(("arbitrary", "arbitrary", "arbitrary")),
        name=name,
    )(a, w, *arrays, *s_arr)
    return outs[0] if len(outs) == 1 else tuple(outs)


def _ep_half_residual(acc, res):
    return res + 0.5 * acc


def _ep_residual(acc, res):
    return res + acc


def _ep_gelu(acc):
    return _gelu(acc)


def _ep_gate(acc, bias):
    return jax.nn.sigmoid(acc + bias)


def _ep_identity(acc):
    return acc


def _ep_kv(acc, gain, cos_t, sin_t):
    k = acc[:, :HEAD_DIM]
    ms = jnp.mean(k * k, axis=-1, keepdims=True)
    k = _rope_pairs(k * lax.rsqrt(ms + EPS) * gain, cos_t, sin_t, HEAD_DIM // 2)
    return jnp.concatenate([k, acc[:, HEAD_DIM:]], axis=-1)


def _ep_small(acc, gain, cos_t, sin_t, *, idx_scale):
    lane = lax.broadcasted_iota(jnp.int32, acc.shape, 1)
    is_k = lane < IDX_DIM
    ms = jnp.sum(jnp.where(is_k, acc * acc, 0.0), axis=-1, keepdims=True) * (1.0 / IDX_DIM)
    k = _rope_pairs(acc * lax.rsqrt(ms + EPS) * gain, cos_t, sin_t, IDX_DIM // 2)
    return jnp.where(is_k, k, acc * idx_scale)


def _ep_keyside(acc, k_gain, cos_a, sin_a, ki_gain, cos_i, sin_i, *, idx_scale):
    kv = _ep_kv(acc[:, :2 * HEAD_DIM], k_gain, cos_a, sin_a)
    small = _ep_small(acc[:, 2 * HEAD_DIM:], ki_gain, cos_i, sin_i, idx_scale=idx_scale)
    return jnp.concatenate([kv, small], axis=-1)


def _gmlp_kernel(u_ref, v_ref, gain_ref, ws_ref, bs_ref, o_ref):
    groups = ws_ref.shape[0]
    v = v_ref[...].astype(f32)
    ms = jnp.mean(v * v, axis=-1, keepdims=True)
    vn = (v * lax.rsqrt(ms + EPS) * gain_ref[...]).astype(bf16)
    row_chunk = lax.broadcasted_iota(jnp.int32, (MIX_BLOCK, MIX_BLOCK), 0) // CHUNK
    col_chunk = lax.broadcasted_iota(jnp.int32, (MIX_BLOCK, MIX_BLOCK), 1) // CHUNK
    causal = col_chunk <= row_chunk
    bs = bs_ref[...]
    for g in range(groups):
        sl = slice(g * GROUP_DIM, (g + 1) * GROUP_DIM)
        w = jnp.where(causal, ws_ref[g], 0.0).astype(bf16)
        mixed = jnp.dot(w, vn[:, sl], preferred_element_type=f32) + bs[:, g:g + 1]
        o_ref[:, sl] = (u_ref[:, sl].astype(f32) * mixed).astype(o_ref.dtype)


def gmlp_gate(uv, gain, ws, bs_t):
    n = uv.shape[0]
    gw = uv.shape[1] // 2
    groups = gw // GROUP_DIM
    return pl.pallas_call(
        _gmlp_kernel,
        grid=(n // MIX_BLOCK,),
        in_specs=[pl.BlockSpec((MIX_BLOCK, gw), lambda i: (i, 0)),
                  pl.BlockSpec((MIX_BLOCK, gw), lambda i: (i, 1)),
                  pl.BlockSpec((1, gw), lambda i: (0, 0)),
                  pl.BlockSpec((groups, MIX_BLOCK, MIX_BLOCK), lambda i: (0, 0, 0)),
                  pl.BlockSpec((MIX_BLOCK, groups), lambda i: (0, 0))],
        out_specs=pl.BlockSpec((MIX_BLOCK, gw), lambda i: (i, 0)),
        out_shape=jax.ShapeDtypeStruct((n, gw), bf16),
        compiler_params=_params(("parallel",)),
        name="gmlp_gate",
    )(uv, uv, gain.reshape(1, gw), ws, bs_t)


def _sortable(x):
    bits = lax.bitcast_convert_type(x, jnp.int32)
    return bits ^ ((bits >> 31) & jnp.int32(0x7FFFFFFF))


def _dsa_kernel(q_ref, qi_ref, ks_ref, ksq_ref, qgain_ref,
                cos_a_ref, sin_a_ref, cos_i_ref, sin_i_ref, o_ref,
                qit_ref, q2t_ref, vt_ref, keys_ref, bias_ref, m_ref, acc_ref,
                *, n_qblk, top):
    qb = pl.program_id(0) % n_qblk
    n_kt = (qb + 1) * (Q_BLOCK // KEY_TILE)
    n_grp = (n_kt + SCORE_GROUP - 1) // SCORE_GROUP
    n_grp_attn = (n_kt + ATTN_GROUP - 1) // ATTN_GROUP
    heads = q_ref.shape[1] // HEAD_DIM
    idx_heads = qi_ref.shape[1] // IDX_DIM

    @pl.when(qb == 0)
    def _():
        def vt_tile(kt, carry):
            base = pl.multiple_of(kt * KEY_TILE, KEY_TILE)
            v_t = ks_ref[pl.ds(base, KEY_TILE), :][:, HEAD_DIM:2 * HEAD_DIM]
            vt_ref[kt, :HEAD_DIM] = jnp.transpose(v_t).astype(bf16)
            vt_ref[kt, HEAD_DIM:] = jnp.ones((DENOM_ROWS, KEY_TILE), bf16)
            return carry

        lax.fori_loop(0, n_qblk * Q_BLOCK // KEY_TILE, vt_tile, 0)

    def rope_rows(x, cos_t, sin_t):
        half = x.shape[0] // 2
        x1, x2 = x[:half], x[half:]
        return jnp.concatenate([x1 * cos_t - x2 * sin_t, x1 * sin_t + x2 * cos_t], axis=0)

    cos_i, sin_i = cos_i_ref[...], sin_i_ref[...]
    for c in range(qi_ref.shape[1] // LANES):
        t = jnp.transpose(qi_ref[:, c * LANES:(c + 1) * LANES])
        for j in range(LANES // IDX_DIM):
            h = c * (LANES // IDX_DIM) + j
            qit_ref[:, h * Q_BLOCK:(h + 1) * Q_BLOCK] = rope_rows(
                t[j * IDX_DIM:(j + 1) * IDX_DIM], cos_i, sin_i).astype(bf16)
    cos_a, sin_a, qgain = cos_a_ref[...], sin_a_ref[...], qgain_ref[...]
    for h in range(heads):
        t = jnp.transpose(q_ref[:, h * HEAD_DIM:(h + 1) * HEAD_DIM])
        ms = jnp.mean(t * t, axis=0, keepdims=True)
        t = rope_rows(t * lax.rsqrt(ms + EPS) * qgain, cos_a, sin_a)
        q2t_ref[:, h * Q_BLOCK:(h + 1) * Q_BLOCK] = (t * (HEAD_DIM ** -0.5 * LOG2_E)).astype(bf16)
    w_t = jnp.transpose(ksq_ref[...])
    q_chunk = (qb * Q_BLOCK + lax.broadcasted_iota(jnp.int32, (KEY_TILE, Q_BLOCK), 1)) // CHUNK
    key_off = lax.broadcasted_iota(jnp.int32, (KEY_TILE, Q_BLOCK), 0)

    def score_group(g, carry):
        rows = SCORE_GROUP * KEY_TILE
        base = pl.multiple_of(g * rows, rows)
        ki = ks_ref[pl.ds(base, rows), :][:, 2 * HEAD_DIM:2 * HEAD_DIM + IDX_DIM].astype(bf16)
        rel = jnp.dot(ki, qit_ref[...], preferred_element_type=f32)
        for u in range(SCORE_GROUP):
            us = slice(u * KEY_TILE, (u + 1) * KEY_TILE)
            s = jnp.zeros((KEY_TILE, Q_BLOCK), f32)
            for h in range(idx_heads):
                r = jnp.maximum(rel[us, h * Q_BLOCK:(h + 1) * Q_BLOCK], 0.0)
                s = s + w_t[IDX_DIM + h:IDX_DIM + h + 1, :] * r
            adm = (base + u * KEY_TILE + key_off) // CHUNK <= q_chunk
            keys_ref[g * SCORE_GROUP + u] = jnp.where(adm, _sortable(s), INT_MIN)
        return carry

    lax.fori_loop(0, n_grp, score_group, 0)

    def count_keys(pred):
        def count_group(g, cnt):
            for u in range(SCORE_GROUP):
                kt = g * SCORE_GROUP + u
                hit = pred(keys_ref[kt], kt * KEY_TILE + key_off).astype(jnp.int32)
                cnt = cnt + jnp.sum(hit.reshape(KEY_TILE // 8, 8, Q_BLOCK), axis=0)
            return cnt

        cnt = lax.fori_loop(0, n_grp, count_group, jnp.zeros((8, Q_BLOCK), jnp.int32))
        return jnp.sum(cnt, axis=0, keepdims=True)

    def bit_step(i, carry):
        prefix, n_ge = carry
        cand = prefix + jnp.left_shift(jnp.int32(1), 31 - i)
        total = count_keys(lambda kk, idx: kk >= cand)
        take = total >= top
        return jnp.where(take, cand, prefix), jnp.where(take, total, n_ge)

    thr, n_ge = lax.fori_loop(
        0, 32, bit_step,
        (jnp.full((1, Q_BLOCK), INT_MIN, jnp.int32), jnp.zeros((1, Q_BLOCK), jnp.int32)))

    def tie_search(_):
        def idx_step(i, bound):
            cand = bound + jnp.left_shift(jnp.int32(1), idx_bits - 1 - i)
            total = count_keys(lambda kk, idx: (kk > thr) | ((kk == thr) & (idx < cand)))
            return jnp.where(total < top, cand, bound)

        return lax.fori_loop(0, idx_bits, idx_step, jnp.zeros((1, Q_BLOCK), jnp.int32))

    idx_bits = int(keys_ref.shape[0] * KEY_TILE).bit_length()
    no_ties = lambda _: jnp.full((1, Q_BLOCK), keys_ref.shape[0] * KEY_TILE, jnp.int32)
    last_idx = lax.cond(jnp.max(n_ge) > top, tie_search, no_ties, 0)

    m_ref[...] = jnp.full(m_ref.shape, NEG_BIG, f32)
    acc_ref[...] = jnp.zeros(acc_ref.shape, f32)

    def attn_group(g, carry):
        rows = ATTN_GROUP * KEY_TILE
        base = pl.multiple_of(g * rows, rows)
        k_t = ks_ref[pl.ds(base, rows), :][:, :HEAD_DIM].astype(bf16)
        sel = []
        for u in range(ATTN_GROUP):
            kk = keys_ref[g * ATTN_GROUP + u]
            idx = base + u * KEY_TILE + key_off
            sel.append(((kk > thr) | ((kk == thr) & (idx <= last_idx))) & (kk != INT_MIN))
        bias_ref[...] = jnp.where(jnp.concatenate(sel, axis=0), 0.0, NEG_BIG)
        v_t = jnp.concatenate([vt_ref[g * ATTN_GROUP + u] for u in range(ATTN_GROUP)], axis=1)
        logits = jnp.dot(k_t, q2t_ref[...], preferred_element_type=f32)
        probs, alphas = [], []
        for h in range(heads):
            hs = slice(h * Q_BLOCK, (h + 1) * Q_BLOCK)
            lg = logits[:, hs] + bias_ref[...]
            m_old = m_ref[:, hs]
            m_new = jnp.maximum(m_old, jnp.max(lg, axis=0, keepdims=True))
            m_ref[:, hs] = m_new
            alphas.append(jnp.exp2(m_old - m_new))
            probs.append(jnp.exp2(lg - m_new).astype(bf16))
        pv = jnp.dot(v_t, jnp.concatenate(probs, axis=1),
                     preferred_element_type=f32)
        acc_ref[...] = acc_ref[...] * jnp.concatenate(alphas, axis=1) + pv
        return carry

    lax.fori_loop(0, n_grp_attn, attn_group, 0)

    out = acc_ref[:HEAD_DIM] / acc_ref[HEAD_DIM:HEAD_DIM + 1]
    for h in range(heads):
        o_ref[:, h * HEAD_DIM:(h + 1) * HEAD_DIM] = jnp.transpose(
            out[:, h * Q_BLOCK:(h + 1) * Q_BLOCK]).astype(o_ref.dtype)


def dsa_attention(q, qi, keyside, q_gain, rope_a, rope_i, batch, seq):
    n, aw = q.shape
    n_qblk = seq // Q_BLOCK
    n_kt = seq // KEY_TILE
    assert n_kt % SCORE_GROUP == 0 and SCORE_GROUP % ATTN_GROUP == 0
    heads = aw // HEAD_DIM
    iw = qi.shape[1]
    top = min(TOPK_MAX, seq // 4)
    q_gain_t = jnp.broadcast_to(q_gain.reshape(HEAD_DIM, 1), (HEAD_DIM, Q_BLOCK))
    per_qblk = lambda rows: pl.BlockSpec((rows, Q_BLOCK), lambda g: (0, g % n_qblk))
    return pl.pallas_call(
        functools.partial(_dsa_kernel, n_qblk=n_qblk, top=top),
        grid=(batch * n_qblk,),
        in_specs=[pl.BlockSpec((Q_BLOCK, aw), lambda g: (g, 0)),
                  pl.BlockSpec((Q_BLOCK, iw), lambda g: (g, 0)),
                  pl.BlockSpec((seq, keyside.shape[1]), lambda g: (g // n_qblk, 0)),
                  pl.BlockSpec((Q_BLOCK, LANES), lambda g: (g, 2 * HEAD_DIM // LANES)),
                  pl.BlockSpec((HEAD_DIM, Q_BLOCK), lambda g: (0, 0)),
                  per_qblk(HEAD_DIM // 2), per_qblk(HEAD_DIM // 2),
                  per_qblk(IDX_DIM // 2), per_qblk(IDX_DIM // 2)],
        out_specs=pl.BlockSpec((Q_BLOCK, aw), lambda g: (g, 0)),
        out_shape=jax.ShapeDtypeStruct((n, aw), bf16),
        scratch_shapes=[pltpu.VMEM((IDX_DIM, (iw // IDX_DIM) * Q_BLOCK), bf16),
                        pltpu.VMEM((HEAD_DIM, heads * Q_BLOCK), bf16),
                        pltpu.VMEM((n_kt, HEAD_DIM + DENOM_ROWS, KEY_TILE), bf16),
                        pltpu.VMEM((n_kt, KEY_TILE, Q_BLOCK), jnp.int32),
                        pltpu.VMEM((ATTN_GROUP * KEY_TILE, Q_BLOCK), f32),
                        pltpu.VMEM((1, heads * Q_BLOCK), f32),
                        pltpu.VMEM((HEAD_DIM + DENOM_ROWS, heads * Q_BLOCK), f32)],
        compiler_params=_params(("arbitrary",)),
        name="dsa_attention",
    )(q, qi, keyside, keyside, q_gain_t, *rope_a, *rope_i)


def _merge_kernel(ya_ref, yb_ref, wa_ref, wb_ref, ga_ref, gb_ref, o_ref):
    a = jnp.dot(ya_ref[...], wa_ref[...], preferred_element_type=f32)
    b = jnp.dot(yb_ref[...], wb_ref[...], preferred_element_type=f32)
    o_ref[...] = (ga_ref[...].astype(f32) * a + gb_ref[...].astype(f32) * b).astype(o_ref.dtype)


def gated_merge(ya, yb, wa, wb, gates, tm=1024, tn=512):
    n, ka = ya.shape
    kb = yb.shape[1]
    d = wa.shape[1]
    tm, tn = _tile(n, tm), _tile(d, tn)
    nj = d // tn
    assert d % tn == 0
    return pl.pallas_call(
        _merge_kernel,
        grid=(pl.cdiv(n, tm), nj),
        in_specs=[pl.BlockSpec((tm, ka), lambda i, j: (i, 0)),
                  pl.BlockSpec((tm, kb), lambda i, j: (i, 0)),
                  pl.BlockSpec((ka, tn), lambda i, j: (0, j)),
                  pl.BlockSpec((kb, tn), lambda i, j: (0, j)),
                  pl.BlockSpec((tm, tn), lambda i, j: (i, j)),
                  pl.BlockSpec((tm, tn), lambda i, j: (i, j + nj))],
        out_specs=pl.BlockSpec((tm, tn), lambda i, j: (i, j)),
        out_shape=jax.ShapeDtypeStruct((n, d), bf16),
        compiler_params=_params(("parallel", "arbitrary")),
        name="gated_merge",
    )(ya, yb, wa, wb, gates, gates)


def _rope_tables(seq, dim, lanes):
    inv = ROPE_THETA ** (-jnp.arange(0, dim, 2, dtype=f32) / dim)
    ang = jnp.arange(seq, dtype=f32)[:, None] * inv[None, :]
    cos, sin = jnp.cos(ang), jnp.sin(ang)
    reps = lanes // dim
    cos_t = jnp.tile(jnp.concatenate([cos, cos], axis=-1), (1, reps))
    sin_t = jnp.tile(jnp.concatenate([-sin, sin], axis=-1), (1, reps))
    return cos_t, sin_t


def _rope_tables_t(seq, dim):
    inv = ROPE_THETA ** (-jnp.arange(0, dim, 2, dtype=f32) / dim)
    ang = inv[:, None] * jnp.arange(seq, dtype=f32)[None, :]
    return jnp.cos(ang), jnp.sin(ang)


def _fold_gain(gain, w):
    return (gain[:, None] * w).astype(bf16)


def _project(h16, h_ss, w16_t, col_off, n_cols, epilogue, out_dtype, tn, **kw):
    tn = min(tn, n_cols)
    if col_off % tn == 0 and n_cols % tn == 0:
        return matmul(h16, w16_t, epilogue, out_dtype, tn=tn, n_cols=n_cols, col_off=col_off,
                      row_sumsq=h_ss, w_rows=True, **kw)
    return matmul(h16, w16_t[col_off:col_off + n_cols], epilogue, out_dtype, tn=tn,
                  row_sumsq=h_ss, w_rows=True, **kw)


def kernel(x, ffn1_norm, ffn1_w1, ffn1_w3, ffn1_w2, mix_norm, w_in, gate_bias,
           gmlp_v_norm, gmlp_ws, gmlp_bs, q_norm, k_norm, idx_k_norm,
           w_br_a, w_br_b, w_out, ffn2_norm, ffn2_w1, ffn2_w3, ffn2_w2):
    batch, seq, d = x.shape
    depth = ffn1_norm.shape[0]
    n = batch * seq
    gw = gmlp_v_norm.shape[1]
    aw = w_br_b.shape[1]
    idx_heads = d // 128
    iw = idx_heads * IDX_DIM
    d_ff = ffn1_w1.shape[2]
    nk_down = 2 if (d_ff % (2 * LANES) == 0 and d_ff > 4096) else 1
    c_uv, c_q, c_kv, c_qi = 2 * gw, aw, 2 * HEAD_DIM, iw
    o_q = c_uv
    o_kv = o_q + c_q
    o_qi = o_kv + c_kv
    o_small = o_qi + c_qi
    o_gate = o_small + IDX_DIM + idx_heads
    assert o_gate + 2 * d == w_in.shape[2]
    assert IDX_DIM + idx_heads <= LANES
    idx_scale = (idx_heads ** -0.5) * (IDX_DIM ** -0.5)

    cos_a, sin_a = _rope_tables(seq, HEAD_DIM, LANES)
    cos_i, sin_i = _rope_tables(seq, IDX_DIM, LANES)
    rope_a_t = _rope_tables_t(seq, HEAD_DIM)
    rope_i_t = _rope_tables_t(seq, IDX_DIM)
    tm = _tile(seq, 1024)
    assert seq % tm == 0 and seq % Q_BLOCK == 0

    h = x.reshape(n, d)
    h16, h_ss = cast_sumsq(h)
    for l in range(depth):
        w_in_t = jnp.transpose(w_in[l])
        g, (w2_16, w_in16_t, wa16, wb16, wo16) = swiglu_up(
            h16, h_ss, _fold_gain(ffn1_norm[l], ffn1_w1[l]), _fold_gain(ffn1_norm[l], ffn1_w3[l]),
            sides=[(ffn1_w2[l], None), (w_in_t, mix_norm[l].reshape(1, d)),
                   (w_br_a[l], None), (w_br_b[l], None), (w_out[l], None)])
        h, h16, h_ss = matmul(g, w2_16, _ep_half_residual, f32, extras=[(h, "tile")],
                              nk=nk_down, emit_norm=True, name="ffn_down")

        ffn2_gain = ffn2_norm[l].reshape(d, 1)
        uv = _project(h16, h_ss, w_in16_t, 0, c_uv, _ep_gelu, bf16, WIDE_TN, tm=tm,
                      name="proj_uv")
        q = _project(h16, h_ss, w_in16_t, o_q, c_q, _ep_identity, f32, WIDE_TN, tm=tm,
                     name="proj_q")
        qi = _project(h16, h_ss, w_in16_t, o_qi, c_qi, _ep_identity, f32, WIDE_TN, tm=tm,
                      name="proj_qi")
        w_keyside = jnp.concatenate(
            [w_in16_t[o_kv:o_kv + c_kv],
             jnp.pad(w_in16_t[o_small:o_gate], ((0, LANES - (o_gate - o_small)), (0, 0)))], axis=0)
        ki_gain = jnp.tile(idx_k_norm[l].reshape(1, IDX_DIM), (1, LANES // IDX_DIM))
        keyside = matmul(h16, w_keyside, functools.partial(_ep_keyside, idx_scale=idx_scale), f32,
                         tm=tm, tn=c_kv + LANES, row_sumsq=h_ss, w_rows=True,
                         extras=[(k_norm[l].reshape(1, HEAD_DIM), "const"), (cos_a, "pos"),
                                 (sin_a, "pos"), (ki_gain, "const"), (cos_i, "pos"),
                                 (sin_i, "pos")], name="proj_keyside")
        gates, w1b_16, w3b_16 = _project(
            h16, h_ss, w_in16_t, o_gate, 2 * d, _ep_gate, bf16, WIDE_TN, tm=tm,
            extras=[(gate_bias[l].reshape(1, 2 * d), "row")],
            sides=[(ffn2_w1[l], ffn2_gain), (ffn2_w3[l], ffn2_gain)], name="proj_gate")

        ya = gmlp_gate(uv, gmlp_v_norm[l], gmlp_ws[l], gmlp_bs[l].T)
        yb = dsa_attention(q, qi, keyside, q_norm[l], rope_a_t, rope_i_t, batch, seq)
        m = gated_merge(ya, yb, wa16, wb16, gates, tm=tm, tn=WIDE_TN)
        h, h16, h_ss = matmul(m, wo16, _ep_residual, f32, extras=[(h, "tile")],
                              tm=tm, emit_norm=True, name="out_proj")

        g, (w2b_16,) = swiglu_up(h16, h_ss, w1b_16, w3b_16, sides=[(ffn2_w2[l], None)])
        last = l == depth - 1
        out = matmul(g, w2b_16, _ep_half_residual, f32, extras=[(h, "tile")], nk=nk_down,
                     emit_norm=not last, name="ffn_down")
        h, h16, h_ss = (out, None, None) if last else out
    return h.reshape(batch, seq, d)
```

```python
import functools
import math

import jax
import jax.numpy as jnp
from jax import lax
from jax.experimental import pallas as pl
from jax.experimental.pallas import tpu as pltpu

CHUNK = 64
GROUP_DIM = 128
MIX_BLOCK = 128
HEAD_DIM = 128
IDX_DIM = 64
Q_BLOCK = 256
KEY_TILE = 128
SCORE_GROUP = 2
ATTN_GROUP = 2
DENOM_ROWS = 16
LOG2_E = math.log2(math.e)
TOPK_MAX = 256
ROPE_THETA = 10000.0
EPS = 1e-6
LANES = 128
BF16_SUBLANES = 16
NEG_BIG = -1e30
INT_MIN = -(2 ** 31)
VMEM_LIMIT_BYTES = 60000 * 1024
WIDE_TN = 1024

f32 = jnp.float32
bf16 = jnp.bfloat16


def _params(sem):
    return pltpu.CompilerParams(dimension_semantics=sem,
                                vmem_limit_bytes=VMEM_LIMIT_BYTES)


def _tile(dim, want):
    return dim if dim <= want else want


def _gelu(x):
    return 0.5 * x * (1.0 + lax.erf(x * (2.0 ** -0.5)))


def _rope_pairs(x, cos_t, sin_t, half):
    width = x.shape[-1]
    lane = lax.broadcasted_iota(jnp.int32, x.shape, x.ndim - 1)
    fwd = pltpu.roll(x, width - half, axis=x.ndim - 1)
    bwd = pltpu.roll(x, half, axis=x.ndim - 1)
    partner = jnp.where((lane % (2 * half)) < half, fwd, bwd)
    return x * cos_t + partner * sin_t


def _row_rsqrt(sumsq, width):
    return lax.rsqrt(sumsq * (1.0 / width) + EPS)


def _cast_sumsq_kernel(x_ref, o_ref, ss_ref):
    x = x_ref[...]
    o_ref[...] = x.astype(o_ref.dtype)
    ss_ref[...] = jnp.sum(x * x, axis=-1, keepdims=True)


def cast_sumsq(x, tm=256):
    n, d = x.shape
    tm = _tile(n, tm)
    return pl.pallas_call(
        _cast_sumsq_kernel,
        grid=(pl.cdiv(n, tm),),
        in_specs=[pl.BlockSpec((tm, d), lambda i: (i, 0))],
        out_specs=[pl.BlockSpec((tm, d), lambda i: (i, 0)),
                   pl.BlockSpec((tm, 1), lambda i: (i, 0))],
        out_shape=[jax.ShapeDtypeStruct((n, d), bf16),
                   jax.ShapeDtypeStruct((n, 1), f32)],
        compiler_params=_params(("parallel",)),
        name="cast_sumsq",
    )(x)


def _side_tile_rows(rows, steps):
    return BF16_SUBLANES * (-(-rows // (BF16_SUBLANES * steps)))


def _side_plan(sides, steps, flat_step):
    in_specs, in_arrays, out_specs, out_shapes, scaled = [], [], [], [], []
    for src, scale, *row_range in sides:
        first_row, rows = row_range if row_range else (0, src.shape[0])
        cols = src.shape[1]
        tr = _side_tile_rows(rows, steps)
        n_tiles = -(-rows // tr)
        assert n_tiles <= steps and first_row % tr == 0
        idx = lambda *g, n_tiles=n_tiles: (jnp.minimum(flat_step(*g), n_tiles - 1), 0)
        src_idx = lambda *g, idx=idx, off=first_row // tr: (idx(*g)[0] + off, 0)
        in_specs.append(pl.BlockSpec((tr, cols), src_idx))
        in_arrays.append(src)
        if scale is not None:
            if scale.shape == (1, cols):
                in_specs.append(pl.BlockSpec((1, cols), lambda *g: (0, 0)))
            else:
                assert scale.shape == (rows, 1) and not row_range
                in_specs.append(pl.BlockSpec((tr, 1), idx))
            in_arrays.append(scale)
        scaled.append(scale is not None)
        out_specs.append(pl.BlockSpec((tr, cols), idx))
        out_shapes.append(jax.ShapeDtypeStruct((rows, cols), bf16))
    return in_specs, in_arrays, out_specs, out_shapes, tuple(scaled)


def _take_side_inputs(it, scaled):
    return [(next(it), next(it) if s else None) for s in scaled]


def _run_sides(side_in, side_out):
    for (src_ref, scale_ref), dst_ref in zip(side_in, side_out):
        v = src_ref[...]
        if scale_ref is not None:
            v = v * scale_ref[...]
        dst_ref[...] = v.astype(dst_ref.dtype)


def _swiglu_up_kernel(*refs, scaled):
    it = iter(refs)
    x_ref, ss_ref, w1_ref, w3_ref = next(it), next(it), next(it), next(it)
    side_in = _take_side_inputs(it, scaled)
    o_ref = next(it)
    side_out = [next(it) for _ in scaled]
    x = x_ref[...]
    r = _row_rsqrt(ss_ref[...], x.shape[1])
    a = jnp.dot(x, w1_ref[...], preferred_element_type=f32) * r
    b = jnp.dot(x, w3_ref[...], preferred_element_type=f32) * r
    o_ref[...] = (a * jax.nn.sigmoid(a) * b).astype(o_ref.dtype)
    _run_sides(side_in, side_out)


def swiglu_up(x16, sumsq, w1, w3, sides=(), tm=1024, tn=512):
    n, d = x16.shape
    f = w1.shape[1]
    tm, tn = _tile(n, tm), _tile(f, tn)
    ni, nj = pl.cdiv(n, tm), pl.cdiv(f, tn)
    s_in, s_arr, s_out, s_shape, scaled = _side_plan(sides, ni * nj, lambda i, j: i * nj + j)
    outs = pl.pallas_call(
        functools.partial(_swiglu_up_kernel, scaled=scaled),
        grid=(ni, nj),
        in_specs=[pl.BlockSpec((tm, d), lambda i, j: (i, 0)),
                  pl.BlockSpec((tm, 1), lambda i, j: (i, 0)),
                  pl.BlockSpec((d, tn), lambda i, j: (0, j)),
                  pl.BlockSpec((d, tn), lambda i, j: (0, j))] + s_in,
        out_specs=[pl.BlockSpec((tm, tn), lambda i, j: (i, j))] + s_out,
        out_shape=[jax.ShapeDtypeStruct((n, f), bf16)] + s_shape,
        compiler_params=_params(("arbitrary", "arbitrary")),
        name="swiglu_up",
    )(x16, sumsq, w1, w3, *s_arr)
    return outs[0], outs[1:]


def _mm_kernel(*refs, nk, kdim, n_extra, epilogue, row_norm, emit_norm, scaled, w_rows):
    it = iter(refs)
    a_ref, w_ref = next(it), next(it)
    ss_in_ref = next(it) if row_norm else None
    extra = [next(it) for _ in range(n_extra)]
    side_in = _take_side_inputs(it, scaled)
    o_ref = next(it)
    o16_ref, ss_out_ref = (next(it), next(it)) if emit_norm else (None, None)
    side_out = [next(it) for _ in scaled]
    acc_ref = next(it) if nk > 1 else None
    j, k = pl.program_id(1), pl.program_id(2)

    if nk > 1:
        @pl.when(k == 0)
        def _():
            acc_ref[...] = jnp.zeros(acc_ref.shape, f32)

    if emit_norm:
        @pl.when((j == 0) & (k == 0))
        def _():
            ss_out_ref[...] = jnp.zeros(ss_out_ref.shape, f32)

    dims = (((1,), (1,)), ((), ())) if w_rows else (((1,), (0,)), ((), ()))
    acc = lax.dot_general(a_ref[...], w_ref[...], dims, preferred_element_type=f32)
    if nk > 1:
        acc = acc_ref[...] + acc
        acc_ref[...] = acc
    if row_norm:
        acc = acc * _row_rsqrt(ss_in_ref[...], kdim)
    out = epilogue(acc, *[e[...] for e in extra])
    o_ref[...] = out.astype(o_ref.dtype)
    if emit_norm:
        o16_ref[...] = out.astype(o16_ref.dtype)
        part_ss = jnp.sum(out * out, axis=-1, keepdims=True)
        ss_out_ref[...] += jnp.where(k == nk - 1, part_ss, 0.0)

    _run_sides(side_in, side_out)


def matmul(a, w, epilogue, out_dtype, extras=(), tm=1024, tn=512, nk=1, name="mm",
           n_cols=None, col_off=0, row_sumsq=None, emit_norm=False, sides=(), w_rows=False):
    m, kdim = a.shape
    n = w.shape[0 if w_rows else 1] if n_cols is None else n_cols
    tm, tn = _tile(m, tm), _tile(n, tn)
    assert kdim % nk == 0 and col_off % tn == 0
    tk = kdim // nk
    assert nk == 1 or tk % LANES == 0
    ni, nj, joff = pl.cdiv(m, tm), pl.cdiv(n, tn), col_off // tn
    in_specs = [pl.BlockSpec((tm, tk), lambda i, j, k: (i, k)),
                pl.BlockSpec((tn, tk), lambda i, j, k: (j + joff, k)) if w_rows else
                pl.BlockSpec((tk, tn), lambda i, j, k: (k, j + joff))]
    arrays = []
    if row_sumsq is not None:
        in_specs.append(pl.BlockSpec((tm, 1), lambda i, j, k: (i, 0)))
        arrays.append(row_sumsq)
    n_extra = len(extras)
    for arr, kind in extras:
        arrays.append(arr)
        if kind == "tile":
            in_specs.append(pl.BlockSpec((tm, tn), lambda i, j, k: (i, j)))
        elif kind == "row":
            in_specs.append(pl.BlockSpec((1, tn), lambda i, j, k: (0, j)))
        elif kind == "const":
            in_specs.append(pl.BlockSpec(arr.shape, lambda i, j, k: (0, 0)))
        elif kind == "pos":
            assert arr.shape[0] % tm == 0
            nper = arr.shape[0] // tm
            in_specs.append(pl.BlockSpec((tm, arr.shape[1]),
                                         lambda i, j, k, nper=nper: (i % nper, 0)))
        else:
            raise ValueError(kind)
    s_in, s_arr, s_out, s_shape, scaled = _side_plan(
        sides, ni * nj * nk, lambda i, j, k: (i * nj + j) * nk + k)
    out_specs = [pl.BlockSpec((tm, tn), lambda i, j, k: (i, j))]
    out_shape = [jax.ShapeDtypeStruct((m, n), out_dtype)]
    if emit_norm:
        out_specs += [pl.BlockSpec((tm, tn), lambda i, j, k: (i, j)),
                      pl.BlockSpec((tm, 1), lambda i, j, k: (i, 0))]
        out_shape += [jax.ShapeDtypeStruct((m, n), bf16), jax.ShapeDtypeStruct((m, 1), f32)]
    scratch = [pltpu.VMEM((tm, tn), f32)] if nk > 1 else []
    outs = pl.pallas_call(
        functools.partial(_mm_kernel, nk=nk, kdim=kdim, n_extra=n_extra, epilogue=epilogue,
                          row_norm=row_sumsq is not None, emit_norm=emit_norm, scaled=scaled,
                          w_rows=w_rows),
        grid=(ni, nj, nk),
        in_specs=in_specs + s_in,
        out_specs=out_specs + s_out,
        out_shape=out_shape + s_shape,
        scratch_shapes=scratch,
        compiler_params=_params(("arbitrary", "arbitrary", "arbitrary")),
        name=name,
    )(a, w, *arrays, *s_arr)
    return outs[0] if len(outs) == 1 else tuple(outs)


def _ep_half_residual(acc, res):
    return res + 0.5 * acc


def _ep_residual(acc, res):
    return res + acc


def _ep_gelu(acc):
    return _gelu(acc)


def _ep_gate(acc, bias):
    return jax.nn.sigmoid(acc + bias)


def _ep_identity(acc):
    return acc


def _ep_kv(acc, gain, cos_t, sin_t):
    k = acc[:, :HEAD_DIM]
    ms = jnp.mean(k * k, axis=-1, keepdims=True)
    k = _rope_pairs(k * lax.rsqrt(ms + EPS) * gain, cos_t, sin_t, HEAD_DIM // 2)
    return jnp.concatenate([k, acc[:, HEAD_DIM:]], axis=-1)


def _ep_small(acc, gain, cos_t, sin_t, *, idx_scale):
    lane = lax.broadcasted_iota(jnp.int32, acc.shape, 1)
    is_k = lane < IDX_DIM
    ms = jnp.sum(jnp.where(is_k, acc * acc, 0.0), axis=-1, keepdims=True) * (1.0 / IDX_DIM)
    k = _rope_pairs(acc * lax.rsqrt(ms + EPS) * gain, cos_t, sin_t, IDX_DIM // 2)
    return jnp.where(is_k, k, acc * idx_scale)


def _ep_keyside(acc, k_gain, cos_a, sin_a, ki_gain, cos_i, sin_i, *, idx_scale):
    kv = _ep_kv(acc[:, :2 * HEAD_DIM], k_gain, cos_a, sin_a)
    small = _ep_small(acc[:, 2 * HEAD_DIM:], ki_gain, cos_i, sin_i, idx_scale=idx_scale)
    return jnp.concatenate([kv, small], axis=-1)


def _gmlp_kernel(u_ref, v_ref, gain_ref, ws_ref, bs_ref, o_ref):
    groups = ws_ref.shape[0]
    v = v_ref[...].astype(f32)
    ms = jnp.mean(v * v, axis=-1, keepdims=True)
    vn = (v * lax.rsqrt(ms + EPS) * gain_ref[...]).astype(bf16)
    row_chunk = lax.broadcasted_iota(jnp.int32, (MIX_BLOCK, MIX_BLOCK), 0) // CHUNK
    col_chunk = lax.broadcasted_iota(jnp.int32, (MIX_BLOCK, MIX_BLOCK), 1) // CHUNK
    causal = col_chunk <= row_chunk
    bs = bs_ref[...]
    for g in range(groups):
        sl = slice(g * GROUP_DIM, (g + 1) * GROUP_DIM)
        w = jnp.where(causal, ws_ref[g], 0.0).astype(bf16)
        mixed = jnp.dot(w, vn[:, sl], preferred_element_type=f32) + bs[:, g:g + 1]
        o_ref[:, sl] = (u_ref[:, sl].astype(f32) * mixed).astype(o_ref.dtype)


def gmlp_gate(uv, gain, ws, bs_t):
    n = uv.shape[0]
    gw = uv.shape[1] // 2
    groups = gw // GROUP_DIM
    return pl.pallas_call(
        _gmlp_kernel,
        grid=(n // MIX_BLOCK,),
        in_specs=[pl.BlockSpec((MIX_BLOCK, gw), lambda i: (i, 0)),
                  pl.BlockSpec((MIX_BLOCK, gw), lambda i: (i, 1)),
                  pl.BlockSpec((1, gw), lambda i: (0, 0)),
                  pl.BlockSpec((groups, MIX_BLOCK, MIX_BLOCK), lambda i: (0, 0, 0)),
                  pl.BlockSpec((MIX_BLOCK, groups), lambda i: (0, 0))],
        out_specs=pl.BlockSpec((MIX_BLOCK, gw), lambda i: (i, 0)),
        out_shape=jax.ShapeDtypeStruct((n, gw), bf16),
        compiler_params=_params(("parallel",)),
        name="gmlp_gate",
    )(uv, uv, gain.reshape(1, gw), ws, bs_t)


def _sortable(x):
    bits = lax.bitcast_convert_type(x, jnp.int32)
    return bits ^ ((bits >> 31) & jnp.int32(0x7FFFFFFF))


def _dsa_kernel(q_ref, qi_ref, ks_ref, ksq_ref, qgain_ref,
                cos_a_ref, sin_a_ref, cos_i_ref, sin_i_ref, o_ref,
                qit_ref, q2t_ref, vt_ref, keys_ref, bias_ref, m_ref, acc_ref,
                *, n_qblk, top):
    qb = pl.program_id(0) % n_qblk
    n_kt = (qb + 1) * (Q_BLOCK // KEY_TILE)
    n_grp = (n_kt + SCORE_GROUP - 1) // SCORE_GROUP
    n_grp_attn = (n_kt + ATTN_GROUP - 1) // ATTN_GROUP
    heads = q_ref.shape[1] // HEAD_DIM
    idx_heads = qi_ref.shape[1] // IDX_DIM

    @pl.when(qb == 0)
    def _():
        def vt_tile(kt, carry):
            base = pl.multiple_of(kt * KEY_TILE, KEY_TILE)
            v_t = ks_ref[pl.ds(base, KEY_TILE), :][:, HEAD_DIM:2 * HEAD_DIM]
            vt_ref[kt, :HEAD_DIM] = jnp.transpose(v_t).astype(bf16)
            vt_ref[kt, HEAD_DIM:] = jnp.ones((DENOM_ROWS, KEY_TILE), bf16)
            return carry

        lax.fori_loop(0, n_qblk * Q_BLOCK // KEY_TILE, vt_tile, 0)

    def rope_rows(x, cos_t, sin_t):
        half = x.shape[0] // 2
        x1, x2 = x[:half], x[half:]
        return jnp.concatenate([x1 * cos_t - x2 * sin_t, x1 * sin_t + x2 * cos_t], axis=0)

    cos_i, sin_i = cos_i_ref[...], sin_i_ref[...]
    for c in range(qi_ref.shape[1] // LANES):
        t = jnp.transpose(qi_ref[:, c * LANES:(c + 1) * LANES])
        for j in range(LANES // IDX_DIM):
            h = c * (LANES // IDX_DIM) + j
            qit_ref[:, h * Q_BLOCK:(h + 1) * Q_BLOCK] = rope_rows(
                t[j * IDX_DIM:(j + 1) * IDX_DIM], cos_i, sin_i).astype(bf16)
    cos_a, sin_a, qgain = cos_a_ref[...], sin_a_ref[...], qgain_ref[...]
    for h in range(heads):
        t = jnp.transpose(q_ref[:, h * HEAD_DIM:(h + 1) * HEAD_DIM])
        ms = jnp.mean(t * t, axis=0, keepdims=True)
        t = rope_rows(t * lax.rsqrt(ms + EPS) * qgain, cos_a, sin_a)
        q2t_ref[:, h * Q_BLOCK:(h + 1) * Q_BLOCK] = (t * (HEAD_DIM ** -0.5 * LOG2_E)).astype(bf16)
    w_t = jnp.transpose(ksq_ref[...])
    q_chunk = (qb * Q_BLOCK + lax.broadcasted_iota(jnp.int32, (KEY_TILE, Q_BLOCK), 1)) // CHUNK
    key_off = lax.broadcasted_iota(jnp.int32, (KEY_TILE, Q_BLOCK), 0)

    def score_group(g, carry):
        rows = SCORE_GROUP * KEY_TILE
        base = pl.multiple_of(g * rows, rows)
        ki = ks_ref[pl.ds(base, rows), :][:, 2 * HEAD_DIM:2 * HEAD_DIM + IDX_DIM].astype(bf16)
        rel = jnp.dot(ki, qit_ref[...], preferred_element_type=f32)
        for u in range(SCORE_GROUP):
            us = slice(u * KEY_TILE, (u + 1) * KEY_TILE)
            s = jnp.zeros((KEY_TILE, Q_BLOCK), f32)
            for h in range(idx_heads):
                r = jnp.maximum(rel[us, h * Q_BLOCK:(h + 1) * Q_BLOCK], 0.0)
                s = s + w_t[IDX_DIM + h:IDX_DIM + h + 1, :] * r
            adm = (base + u * KEY_TILE + key_off) // CHUNK <= q_chunk
            keys_ref[g * SCORE_GROUP + u] = jnp.where(adm, _sortable(s), INT_MIN)
        return carry

    lax.fori_loop(0, n_grp, score_group, 0)

    def count_keys(pred):
        def count_group(g, cnt):
            for u in range(SCORE_GROUP):
                kt = g * SCORE_GROUP + u
                hit = pred(keys_ref[kt], kt * KEY_TILE + key_off).astype(jnp.int32)
                cnt = cnt + jnp.sum(hit.reshape(KEY_TILE // 8, 8, Q_BLOCK), axis=0)
            return cnt

        cnt = lax.fori_loop(0, n_grp, count_group, jnp.zeros((8, Q_BLOCK), jnp.int32))
        return jnp.sum(cnt, axis=0, keepdims=True)

    def bit_step(i, carry):
        prefix, n_ge = carry
        cand = prefix + jnp.left_shift(jnp.int32(1), 31 - i)
        total = count_keys(lambda kk, idx: kk >= cand)
        take = total >= top
        return jnp.where(take, cand, prefix), jnp.where(take, total, n_ge)

    thr, n_ge = lax.fori_loop(
        0, 32, bit_step,
        (jnp.full((1, Q_BLOCK), INT_MIN, jnp.int32), jnp.zeros((1, Q_BLOCK), jnp.int32)))

    def tie_search(_):
        def idx_step(i, bound):
            cand = bound + jnp.left_shift(jnp.int32(1), idx_bits - 1 - i)
            total = count_keys(lambda kk, idx: (kk > thr) | ((kk == thr) & (idx < cand)))
            return jnp.where(total < top, cand, bound)

        return lax.fori_loop(0, idx_bits, idx_step, jnp.zeros((1, Q_BLOCK), jnp.int32))

    idx_bits = int(keys_ref.shape[0] * KEY_TILE).bit_length()
    no_ties = lambda _: jnp.full((1, Q_BLOCK), keys_ref.shape[0] * KEY_TILE, jnp.int32)
    last_idx = lax.cond(jnp.max(n_ge) > top, tie_search, no_ties, 0)

    m_ref[...] = jnp.full(m_ref.shape, NEG_BIG, f32)
    acc_ref[...] = jnp.zeros(acc_ref.shape, f32)

    def attn_group(g, carry):
        rows = ATTN_GROUP * KEY_TILE
        base = pl.multiple_of(g * rows, rows)
        k_t = ks_ref[pl.ds(base, rows), :][:, :HEAD_DIM].astype(bf16)
        sel = []
        for u in range(ATTN_GROUP):
            kk = keys_ref[g * ATTN_GROUP + u]
            idx = base + u * KEY_TILE + key_off
            sel.append(((kk > thr) | ((kk == thr) & (idx <= last_idx))) & (kk != INT_MIN))
        bias_ref[...] = jnp.where(jnp.concatenate(sel, axis=0), 0.0, NEG_BIG)
        v_t = jnp.concatenate([vt_ref[g * ATTN_GROUP + u] for u in range(ATTN_GROUP)], axis=1)
        logits = jnp.dot(k_t, q2t_ref[...], preferred_element_type=f32)
        probs, alphas = [], []
        for h in range(heads):
            hs = slice(h * Q_BLOCK, (h + 1) * Q_BLOCK)
            lg = logits[:, hs] + bias_ref[...]
            m_old = m_ref[:, hs]
            m_new = jnp.maximum(m_old, jnp.max(lg, axis=0, keepdims=True))
            m_ref[:, hs] = m_new
            alphas.append(jnp.exp2(m_old - m_new))
            probs.append(jnp.exp2(lg - m_new).astype(bf16))
        pv = jnp.dot(v_t, jnp.concatenate(probs, axis=1),
                     preferred_element_type=f32)
        acc_ref[...] = acc_ref[...] * jnp.concatenate(alphas, axis=1) + pv
        return carry

    lax.fori_loop(0, n_grp_attn, attn_group, 0)

    out = acc_ref[:HEAD_DIM] / acc_ref[HEAD_DIM:HEAD_DIM + 1]
    for h in range(heads):
        o_ref[:, h * HEAD_DIM:(h + 1) * HEAD_DIM] = jnp.transpose(
            out[:, h * Q_BLOCK:(h + 1) * Q_BLOCK]).astype(o_ref.dtype)


def dsa_attention(q, qi, keyside, q_gain, rope_a, rope_i, batch, seq):
    n, aw = q.shape
    n_qblk = seq // Q_BLOCK
    n_kt = seq // KEY_TILE
    assert n_kt % SCORE_GROUP == 0 and SCORE_GROUP % ATTN_GROUP == 0
    heads = aw // HEAD_DIM
    iw = qi.shape[1]
    top = min(TOPK_MAX, seq // 4)
    q_gain_t = jnp.broadcast_to(q_gain.reshape(HEAD_DIM, 1), (HEAD_DIM, Q_BLOCK))
    per_qblk = lambda rows: pl.BlockSpec((rows, Q_BLOCK), lambda g: (0, g % n_qblk))
    return pl.pallas_call(
        functools.partial(_dsa_kernel, n_qblk=n_qblk, top=top),
        grid=(batch * n_qblk,),
        in_specs=[pl.BlockSpec((Q_BLOCK, aw), lambda g: (g, 0)),
                  pl.BlockSpec((Q_BLOCK, iw), lambda g: (g, 0)),
                  pl.BlockSpec((seq, keyside.shape[1]), lambda g: (g // n_qblk, 0)),
                  pl.BlockSpec((Q_BLOCK, LANES), lambda g: (g, 2 * HEAD_DIM // LANES)),
                  pl.BlockSpec((HEAD_DIM, Q_BLOCK), lambda g: (0, 0)),
                  per_qblk(HEAD_DIM // 2), per_qblk(HEAD_DIM // 2),
                  per_qblk(IDX_DIM // 2), per_qblk(IDX_DIM // 2)],
        out_specs=pl.BlockSpec((Q_BLOCK, aw), lambda g: (g, 0)),
        out_shape=jax.ShapeDtypeStruct((n, aw), bf16),
        scratch_shapes=[pltpu.VMEM((IDX_DIM, (iw // IDX_DIM) * Q_BLOCK), bf16),
                        pltpu.VMEM((HEAD_DIM, heads * Q_BLOCK), bf16),
                        pltpu.VMEM((n_kt, HEAD_DIM + DENOM_ROWS, KEY_TILE), bf16),
                        pltpu.VMEM((n_kt, KEY_TILE, Q_BLOCK), jnp.int32),
                        pltpu.VMEM((ATTN_GROUP * KEY_TILE, Q_BLOCK), f32),
                        pltpu.VMEM((1, heads * Q_BLOCK), f32),
                        pltpu.VMEM((HEAD_DIM + DENOM_ROWS, heads * Q_BLOCK), f32)],
        compiler_params=_params(("arbitrary",)),
        name="dsa_attention",
    )(q, qi, keyside, keyside, q_gain_t, *rope_a, *rope_i)


def _merge_kernel(ya_ref, yb_ref, wa_ref, wb_ref, ga_ref, gb_ref, o_ref):
    a = jnp.dot(ya_ref[...], wa_ref[...], preferred_element_type=f32)
    b = jnp.dot(yb_ref[...], wb_ref[...], preferred_element_type=f32)
    o_ref[...] = (ga_ref[...].astype(f32) * a + gb_ref[...].astype(f32) * b).astype(o_ref.dtype)


def gated_merge(ya, yb, wa, wb, gates, tm=1024, tn=512):
    n, ka = ya.shape
    kb = yb.shape[1]
    d = wa.shape[1]
    tm, tn = _tile(n, tm), _tile(d, tn)
    nj = d // tn
    assert d % tn == 0
    return pl.pallas_call(
        _merge_kernel,
        grid=(pl.cdiv(n, tm), nj),
        in_specs=[pl.BlockSpec((tm, ka), lambda i, j: (i, 0)),
                  pl.BlockSpec((tm, kb), lambda i, j: (i, 0)),
                  pl.BlockSpec((ka, tn), lambda i, j: (0, j)),
                  pl.BlockSpec((kb, tn), lambda i, j: (0, j)),
                  pl.BlockSpec((tm, tn), lambda i, j: (i, j)),
                  pl.BlockSpec((tm, tn), lambda i, j: (i, j + nj))],
        out_specs=pl.BlockSpec((tm, tn), lambda i, j: (i, j)),
        out_shape=jax.ShapeDtypeStruct((n, d), bf16),
        compiler_params=_params(("parallel", "arbitrary")),
        name="gated_merge",
    )(ya, yb, wa, wb, gates, gates)


def _rope_tables(seq, dim, lanes):
    inv = ROPE_THETA ** (-jnp.arange(0, dim, 2, dtype=f32) / dim)
    ang = jnp.arange(seq, dtype=f32)[:, None] * inv[None, :]
    cos, sin = jnp.cos(ang), jnp.sin(ang)
    reps = lanes // dim
    cos_t = jnp.tile(jnp.concatenate([cos, cos], axis=-1), (1, reps))
    sin_t = jnp.tile(jnp.concatenate([-sin, sin], axis=-1), (1, reps))
    return cos_t, sin_t


def _rope_tables_t(seq, dim):
    inv = ROPE_THETA ** (-jnp.arange(0, dim, 2, dtype=f32) / dim)
    ang = inv[:, None] * jnp.arange(seq, dtype=f32)[None, :]
    return jnp.cos(ang), jnp.sin(ang)


def _fold_gain(gain, w):
    return (gain[:, None] * w).astype(bf16)


def _project(h16, h_ss, w16_t, col_off, n_cols, epilogue, out_dtype, tn, **kw):
    tn = min(tn, n_cols)
    if col_off % tn == 0 and n_cols % tn == 0:
        return matmul(h16, w16_t, epilogue, out_dtype, tn=tn, n_cols=n_cols, col_off=col_off,
                      row_sumsq=h_ss, w_rows=True, **kw)
    return matmul(h16, w16_t[col_off:col_off + n_cols], epilogue, out_dtype, tn=tn,
                  row_sumsq=h_ss, w_rows=True, **kw)


def kernel(x, ffn1_norm, ffn1_w1, ffn1_w3, ffn1_w2, mix_norm, w_in, gate_bias,
           gmlp_v_norm, gmlp_ws, gmlp_bs, q_norm, k_norm, idx_k_norm,
           w_br_a, w_br_b, w_out, ffn2_norm, ffn2_w1, ffn2_w3, ffn2_w2):
    batch, seq, d = x.shape
    depth = ffn1_norm.shape[0]
    n = batch * seq
    gw = gmlp_v_norm.shape[1]
    aw = w_br_b.shape[1]
    idx_heads = d // 128
    iw = idx_heads * IDX_DIM
    d_ff = ffn1_w1.shape[2]
    nk_down = 2 if (d_ff % (2 * LANES) == 0 and d_ff > 4096) else 1
    c_uv, c_q, c_kv, c_qi = 2 * gw, aw, 2 * HEAD_DIM, iw
    o_q = c_uv
    o_kv = o_q + c_q
    o_qi = o_kv + c_kv
    o_small = o_qi + c_qi
    o_gate = o_small + IDX_DIM + idx_heads
    assert o_gate + 2 * d == w_in.shape[2]
    assert IDX_DIM + idx_heads <= LANES
    idx_scale = (idx_heads ** -0.5) * (IDX_DIM ** -0.5)

    cos_a, sin_a = _rope_tables(seq, HEAD_DIM, LANES)
    cos_i, sin_i = _rope_tables(seq, IDX_DIM, LANES)
    rope_a_t = _rope_tables_t(seq, HEAD_DIM)
    rope_i_t = _rope_tables_t(seq, IDX_DIM)
    tm = _tile(seq, 1024)
    assert seq % tm == 0 and seq % Q_BLOCK == 0

    h = x.reshape(n, d)
    h16, h_ss = cast_sumsq(h)
    for l in range(depth):
        w_in_t = jnp.transpose(w_in[l])
        mix_gain = mix_norm[l].reshape(1, d)
        up_steps = pl.cdiv(n, min(n, 1024)) * pl.cdiv(d_ff, min(d_ff, 512))
        split_gate = o_gate % _side_tile_rows(2 * d, up_steps) == 0
        w_in_sides = ([(w_in_t, mix_gain, 0, o_gate), (w_in_t, mix_gain, o_gate, 2 * d)]
                      if split_gate else [(w_in_t, mix_gain)])
        g, (w2_16, wa16, wb16, wo16, *w_in16) = swiglu_up(
            h16, h_ss, _fold_gain(ffn1_norm[l], ffn1_w1[l]), _fold_gain(ffn1_norm[l], ffn1_w3[l]),
            sides=[(ffn1_w2[l], None), (w_br_a[l], None), (w_br_b[l], None), (w_out[l], None)]
            + w_in_sides)
        w_in16_t = w_in16[0]
        w_gate16_t = w_in16[1] if split_gate else w_in16_t[o_gate:]
        h, h16, h_ss = matmul(g, w2_16, _ep_half_residual, f32, extras=[(h, "tile")],
                              nk=nk_down, emit_norm=True, name="ffn_down")

        ffn2_gain = ffn2_norm[l].reshape(d, 1)
        uv = _project(h16, h_ss, w_in16_t, 0, c_uv, _ep_gelu, bf16, WIDE_TN, tm=tm,
                      name="proj_uv")
        q = _project(h16, h_ss, w_in16_t, o_q, c_q, _ep_identity, f32, WIDE_TN, tm=tm,
                     name="proj_q")
        qi = _project(h16, h_ss, w_in16_t, o_qi, c_qi, _ep_identity, f32, WIDE_TN, tm=tm,
                      name="proj_qi")
        w_keyside = jnp.concatenate(
            [w_in16_t[o_kv:o_kv + c_kv],
             jnp.pad(w_in16_t[o_small:o_gate], ((0, LANES - (o_gate - o_small)), (0, 0)))], axis=0)
        ki_gain = jnp.tile(idx_k_norm[l].reshape(1, IDX_DIM), (1, LANES // IDX_DIM))
        keyside = matmul(h16, w_keyside, functools.partial(_ep_keyside, idx_scale=idx_scale), f32,
                         tm=tm, tn=c_kv + LANES, row_sumsq=h_ss, w_rows=True,
                         extras=[(k_norm[l].reshape(1, HEAD_DIM), "const"), (cos_a, "pos"),
                                 (sin_a, "pos"), (ki_gain, "const"), (cos_i, "pos"),
                                 (sin_i, "pos")], name="proj_keyside")
        gates, w1b_16, w3b_16 = _project(
            h16, h_ss, w_gate16_t, 0, 2 * d, _ep_gate, bf16, WIDE_TN, tm=tm,
            extras=[(gate_bias[l].reshape(1, 2 * d), "row")],
            sides=[(ffn2_w1[l], ffn2_gain), (ffn2_w3[l], ffn2_gain)], name="proj_gate")

        ya = gmlp_gate(uv, gmlp_v_norm[l], gmlp_ws[l], gmlp_bs[l].T)
        yb = dsa_attention(q, qi, keyside, q_norm[l], rope_a_t, rope_i_t, batch, seq)
        m = gated_merge(ya, yb, wa16, wb16, gates, tm=tm, tn=WIDE_TN)
        h, h16, h_ss = matmul(m, wo16, _ep_residual, f32, extras=[(h, "tile")],
                              tm=tm, emit_norm=True, name="out_proj")

        g, (w2b_16,) = swiglu_up(h16, h_ss, w1b_16, w3b_16, sides=[(ffn2_w2[l], None)])
        last = l == depth - 1
        out = matmul(g, w2b_16, _ep_half_residual, f32, extras=[(h, "tile")], nk=nk_down,
                     emit_norm=not last, name="ffn_down")
        h, h16, h_ss = (out, None, None) if last else out
    return h.reshape(batch, seq, d)
```

```python
import functools
import math

import jax
import jax.numpy as jnp
from jax import lax
from jax.experimental import pallas as pl
from jax.experimental.pallas import tpu as pltpu

CHUNK = 64
GROUP_DIM = 128
MIX_BLOCK = 128
GMLP_BLOCKS_PER_STEP = 4
HEAD_DIM = 128
IDX_DIM = 64
Q_BLOCK = 256
KEY_TILE = 128
SCORE_GROUP = 2
ATTN_GROUP = 2
DENOM_ROWS = 16
LOG2_E = math.log2(math.e)
TOPK_MAX = 256
ROPE_THETA = 10000.0
EPS = 1e-6
LANES = 128
BF16_SUBLANES = 16
NEG_BIG = -1e30
INT_MIN = -(2 ** 31)
VMEM_LIMIT_BYTES = 60000 * 1024
WIDE_TN = 1024

f32 = jnp.float32
bf16 = jnp.bfloat16


def _params(sem):
    return pltpu.CompilerParams(dimension_semantics=sem,
                                vmem_limit_bytes=VMEM_LIMIT_BYTES)


def _tile(dim, want):
    return dim if dim <= want else want


def _gelu(x):
    return 0.5 * x * (1.0 + lax.erf(x * (2.0 ** -0.5)))


def _rope_pairs(x, cos_t, sin_t, half):
    width = x.shape[-1]
    lane = lax.broadcasted_iota(jnp.int32, x.shape, x.ndim - 1)
    fwd = pltpu.roll(x, width - half, axis=x.ndim - 1)
    bwd = pltpu.roll(x, half, axis=x.ndim - 1)
    partner = jnp.where((lane % (2 * half)) < half, fwd, bwd)
    return x * cos_t + partner * sin_t


def _row_rsqrt(sumsq, width):
    return lax.rsqrt(sumsq * (1.0 / width) + EPS)


def _cast_sumsq_kernel(x_ref, o_ref, ss_ref):
    x = x_ref[...]
    o_ref[...] = x.astype(o_ref.dtype)
    ss_ref[...] = jnp.sum(x * x, axis=-1, keepdims=True)


def cast_sumsq(x, tm=256):
    n, d = x.shape
    tm = _tile(n, tm)
    return pl.pallas_call(
        _cast_sumsq_kernel,
        grid=(pl.cdiv(n, tm),),
        in_specs=[pl.BlockSpec((tm, d), lambda i: (i, 0))],
        out_specs=[pl.BlockSpec((tm, d), lambda i: (i, 0)),
                   pl.BlockSpec((tm, 1), lambda i: (i, 0))],
        out_shape=[jax.ShapeDtypeStruct((n, d), bf16),
                   jax.ShapeDtypeStruct((n, 1), f32)],
        compiler_params=_params(("parallel",)),
        name="cast_sumsq",
    )(x)


def _side_tile_rows(rows, steps):
    return BF16_SUBLANES * (-(-rows // (BF16_SUBLANES * steps)))


def _side_plan(sides, steps, flat_step):
    in_specs, in_arrays, out_specs, out_shapes, scaled = [], [], [], [], []
    for src, scale, *row_range in sides:
        first_row, rows = row_range if row_range else (0, src.shape[0])
        cols = src.shape[1]
        tr = _side_tile_rows(rows, steps)
        n_tiles = -(-rows // tr)
        assert n_tiles <= steps and first_row % tr == 0
        idx = lambda *g, n_tiles=n_tiles: (jnp.minimum(flat_step(*g), n_tiles - 1), 0)
        src_idx = lambda *g, idx=idx, off=first_row // tr: (idx(*g)[0] + off, 0)
        in_specs.append(pl.BlockSpec((tr, cols), src_idx))
        in_arrays.append(src)
        if scale is not None:
            if scale.shape == (1, cols):
                in_specs.append(pl.BlockSpec((1, cols), lambda *g: (0, 0)))
            else:
                assert scale.shape == (rows, 1) and not row_range
                in_specs.append(pl.BlockSpec((tr, 1), idx))
            in_arrays.append(scale)
        scaled.append(scale is not None)
        out_specs.append(pl.BlockSpec((tr, cols), idx))
        out_shapes.append(jax.ShapeDtypeStruct((rows, cols), bf16))
    return in_specs, in_arrays, out_specs, out_shapes, tuple(scaled)


def _take_side_inputs(it, scaled):
    return [(next(it), next(it) if s else None) for s in scaled]


def _run_sides(side_in, side_out):
    for (src_ref, scale_ref), dst_ref in zip(side_in, side_out):
        v = src_ref[...]
        if scale_ref is not None:
            v = v * scale_ref[...]
        dst_ref[...] = v.astype(dst_ref.dtype)


def _swiglu_up_kernel(*refs, scaled):
    it = iter(refs)
    x_ref, ss_ref, w1_ref, w3_ref = next(it), next(it), next(it), next(it)
    side_in = _take_side_inputs(it, scaled)
    o_ref = next(it)
    side_out = [next(it) for _ in scaled]
    x = x_ref[...]
    r = _row_rsqrt(ss_ref[...], x.shape[1])
    a = jnp.dot(x, w1_ref[...], preferred_element_type=f32) * r
    b = jnp.dot(x, w3_ref[...], preferred_element_type=f32) * r
    o_ref[...] = (a * jax.nn.sigmoid(a) * b).astype(o_ref.dtype)
    _run_sides(side_in, side_out)


def swiglu_up(x16, sumsq, w1, w3, sides=(), tm=1024, tn=512):
    n, d = x16.shape
    f = w1.shape[1]
    tm, tn = _tile(n, tm), _tile(f, tn)
    ni, nj = pl.cdiv(n, tm), pl.cdiv(f, tn)
    s_in, s_arr, s_out, s_shape, scaled = _side_plan(sides, ni * nj, lambda i, j: i * nj + j)
    outs = pl.pallas_call(
        functools.partial(_swiglu_up_kernel, scaled=scaled),
        grid=(ni, nj),
        in_specs=[pl.BlockSpec((tm, d), lambda i, j: (i, 0)),
                  pl.BlockSpec((tm, 1), lambda i, j: (i, 0)),
                  pl.BlockSpec((d, tn), lambda i, j: (0, j)),
                  pl.BlockSpec((d, tn), lambda i, j: (0, j))] + s_in,
        out_specs=[pl.BlockSpec((tm, tn), lambda i, j: (i, j))] + s_out,
        out_shape=[jax.ShapeDtypeStruct((n, f), bf16)] + s_shape,
        compiler_params=_params(("arbitrary", "arbitrary")),
        name="swiglu_up",
    )(x16, sumsq, w1, w3, *s_arr)
    return outs[0], outs[1:]


def _mm_kernel(*refs, nk, kdim, n_extra, epilogue, row_norm, emit_norm, scaled, w_rows):
    it = iter(refs)
    a_ref, w_ref = next(it), next(it)
    ss_in_ref = next(it) if row_norm else None
    extra = [next(it) for _ in range(n_extra)]
    side_in = _take_side_inputs(it, scaled)
    o_ref = next(it)
    o16_ref, ss_out_ref = (next(it), next(it)) if emit_norm else (None, None)
    side_out = [next(it) for _ in scaled]
    acc_ref = next(it) if nk > 1 else None
    j, k = pl.program_id(1), pl.program_id(2)

    if nk > 1:
        @pl.when(k == 0)
        def _():
            acc_ref[...] = jnp.zeros(acc_ref.shape, f32)

    if emit_norm:
        @pl.when((j == 0) & (k == 0))
        def _():
            ss_out_ref[...] = jnp.zeros(ss_out_ref.shape, f32)

    dims = (((1,), (1,)), ((), ())) if w_rows else (((1,), (0,)), ((), ()))
    acc = lax.dot_general(a_ref[...], w_ref[...], dims, preferred_element_type=f32)
    if nk > 1:
        acc = acc_ref[...] + acc
        acc_ref[...] = acc
    if row_norm:
        acc = acc * _row_rsqrt(ss_in_ref[...], kdim)
    out = epilogue(acc, *[e[...] for e in extra])
    o_ref[...] = out.astype(o_ref.dtype)
    if emit_norm:
        o16_ref[...] = out.astype(o16_ref.dtype)
        part_ss = jnp.sum(out * out, axis=-1, keepdims=True)
        ss_out_ref[...] += jnp.where(k == nk - 1, part_ss, 0.0)

    _run_sides(side_in, side_out)


def matmul(a, w, epilogue, out_dtype, extras=(), tm=1024, tn=512, nk=1, name="mm",
           n_cols=None, col_off=0, row_sumsq=None, emit_norm=False, sides=(), w_rows=False):
    m, kdim = a.shape
    n = w.shape[0 if w_rows else 1] if n_cols is None else n_cols
    tm, tn = _tile(m, tm), _tile(n, tn)
    assert kdim % nk == 0 and col_off % tn == 0
    tk = kdim // nk
    assert nk == 1 or tk % LANES == 0
    ni, nj, joff = pl.cdiv(m, tm), pl.cdiv(n, tn), col_off // tn
    in_specs = [pl.BlockSpec((tm, tk), lambda i, j, k: (i, k)),
                pl.BlockSpec((tn, tk), lambda i, j, k: (j + joff, k)) if w_rows else
                pl.BlockSpec((tk, tn), lambda i, j, k: (k, j + joff))]
    arrays = []
    if row_sumsq is not None:
        in_specs.append(pl.BlockSpec((tm, 1), lambda i, j, k: (i, 0)))
        arrays.append(row_sumsq)
    n_extra = len(extras)
    for arr, kind in extras:
        arrays.append(arr)
        if kind == "tile":
            in_specs.append(pl.BlockSpec((tm, tn), lambda i, j, k: (i, j)))
        elif kind == "row":
            in_specs.append(pl.BlockSpec((1, tn), lambda i, j, k: (0, j)))
        elif kind == "const":
            in_specs.append(pl.BlockSpec(arr.shape, lambda i, j, k: (0, 0)))
        elif kind == "pos":
            assert arr.shape[0] % tm == 0
            nper = arr.shape[0] // tm
            in_specs.append(pl.BlockSpec((tm, arr.shape[1]),
                                         lambda i, j, k, nper=nper: (i % nper, 0)))
        else:
            raise ValueError(kind)
    s_in, s_arr, s_out, s_shape, scaled = _side_plan(
        sides, ni * nj * nk, lambda i, j, k: (i * nj + j) * nk + k)
    out_specs = [pl.BlockSpec((tm, tn), lambda i, j, k: (i, j))]
    out_shape = [jax.ShapeDtypeStruct((m, n), out_dtype)]
    if emit_norm:
        out_specs += [pl.BlockSpec((tm, tn), lambda i, j, k: (i, j)),
                      pl.BlockSpec((tm, 1), lambda i, j, k: (i, 0))]
        out_shape += [jax.ShapeDtypeStruct((m, n), bf16), jax.ShapeDtypeStruct((m, 1), f32)]
    scratch = [pltpu.VMEM((tm, tn), f32)] if nk > 1 else []
    outs = pl.pallas_call(
        functools.partial(_mm_kernel, nk=nk, kdim=kdim, n_extra=n_extra, epilogue=epilogue,
                          row_norm=row_sumsq is not None, emit_norm=emit_norm, scaled=scaled,
                          w_rows=w_rows),
        grid=(ni, nj, nk),
        in_specs=in_specs + s_in,
        out_specs=out_specs + s_out,
        out_shape=out_shape + s_shape,
        scratch_shapes=scratch,
        compiler_params=_params(("arbitrary", "arbitrary", "arbitrary")),
        name=name,
    )(a, w, *arrays, *s_arr)
    return outs[0] if len(outs) == 1 else tuple(outs)


def _ep_half_residual(acc, res):
    return res + 0.5 * acc


def _ep_residual(acc, res):
    return res + acc


def _ep_gelu(acc):
    return _gelu(acc)


def _ep_gate(acc, bias):
    return jax.nn.sigmoid(acc + bias)


def _ep_identity(acc):
    return acc


def _ep_kv(acc, gain, cos_t, sin_t):
    k = acc[:, :HEAD_DIM]
    ms = jnp.mean(k * k, axis=-1, keepdims=True)
    k = _rope_pairs(k * lax.rsqrt(ms + EPS) * gain, cos_t, sin_t, HEAD_DIM // 2)
    return jnp.concatenate([k, acc[:, HEAD_DIM:]], axis=-1)


def _ep_small(acc, gain, cos_t, sin_t, *, idx_scale):
    lane = lax.broadcasted_iota(jnp.int32, acc.shape, 1)
    is_k = lane < IDX_DIM
    ms = jnp.sum(jnp.where(is_k, acc * acc, 0.0), axis=-1, keepdims=True) * (1.0 / IDX_DIM)
    k = _rope_pairs(acc * lax.rsqrt(ms + EPS) * gain, cos_t, sin_t, IDX_DIM // 2)
    return jnp.where(is_k, k, acc * idx_scale)


def _ep_keyside(acc, k_gain, cos_a, sin_a, ki_gain, cos_i, sin_i, *, idx_scale):
    kv = _ep_kv(acc[:, :2 * HEAD_DIM], k_gain, cos_a, sin_a)
    small = _ep_small(acc[:, 2 * HEAD_DIM:], ki_gain, cos_i, sin_i, idx_scale=idx_scale)
    return jnp.concatenate([kv, small], axis=-1)


def _gmlp_kernel(u_ref, v_ref, gain_ref, ws_ref, bs_ref, o_ref):
    groups = ws_ref.shape[0]
    v = v_ref[...].astype(f32)
    ms = jnp.mean(v * v, axis=-1, keepdims=True)
    vn = (v * lax.rsqrt(ms + EPS) * gain_ref[...]).astype(bf16)
    row_chunk = lax.broadcasted_iota(jnp.int32, (MIX_BLOCK, MIX_BLOCK), 0) // CHUNK
    col_chunk = lax.broadcasted_iota(jnp.int32, (MIX_BLOCK, MIX_BLOCK), 1) // CHUNK
    causal = col_chunk <= row_chunk
    bs = bs_ref[...]
    for g in range(groups):
        sl = slice(g * GROUP_DIM, (g + 1) * GROUP_DIM)
        w = jnp.where(causal, ws_ref[g], 0.0).astype(bf16)
        for b in range(v_ref.shape[0] // MIX_BLOCK):
            rows = slice(b * MIX_BLOCK, (b + 1) * MIX_BLOCK)
            mixed = jnp.dot(w, vn[rows, sl], preferred_element_type=f32) + bs[:, g:g + 1]
            o_ref[rows, sl] = (u_ref[rows, sl].astype(f32) * mixed).astype(o_ref.dtype)


def gmlp_gate(uv, gain, ws, bs_t):
    n = uv.shape[0]
    gw = uv.shape[1] // 2
    groups = gw // GROUP_DIM
    rows = MIX_BLOCK * GMLP_BLOCKS_PER_STEP
    assert n % rows == 0
    return pl.pallas_call(
        _gmlp_kernel,
        grid=(n // rows,),
        in_specs=[pl.BlockSpec((rows, gw), lambda i: (i, 0)),
                  pl.BlockSpec((rows, gw), lambda i: (i, 1)),
                  pl.BlockSpec((1, gw), lambda i: (0, 0)),
                  pl.BlockSpec((groups, MIX_BLOCK, MIX_BLOCK), lambda i: (0, 0, 0)),
                  pl.BlockSpec((MIX_BLOCK, groups), lambda i: (0, 0))],
        out_specs=pl.BlockSpec((rows, gw), lambda i: (i, 0)),
        out_shape=jax.ShapeDtypeStruct((n, gw), bf16),
        compiler_params=_params(("parallel",)),
        name="gmlp_gate",
    )(uv, uv, gain.reshape(1, gw), ws, bs_t)


def _sortable(x):
    bits = lax.bitcast_convert_type(x, jnp.int32)
    return bits ^ ((bits >> 31) & jnp.int32(0x7FFFFFFF))


def _dsa_kernel(q_ref, qi_ref, ks_ref, ksq_ref, qgain_ref,
                cos_a_ref, sin_a_ref, cos_i_ref, sin_i_ref, o_ref,
                qit_ref, q2t_ref, vt_ref, keys_ref, bias_ref, m_ref, acc_ref,
                *, n_qblk, top):
    qb = pl.program_id(0) % n_qblk
    n_kt = (qb + 1) * (Q_BLOCK // KEY_TILE)
    n_grp = (n_kt + SCORE_GROUP - 1) // SCORE_GROUP
    n_grp_attn = (n_kt + ATTN_GROUP - 1) // ATTN_GROUP
    heads = q_ref.shape[1] // HEAD_DIM
    idx_heads = qi_ref.shape[1] // IDX_DIM

    @pl.when(qb == 0)
    def _():
        def vt_tile(kt, carry):
            base = pl.multiple_of(kt * KEY_TILE, KEY_TILE)
            v_t = ks_ref[pl.ds(base, KEY_TILE), :][:, HEAD_DIM:2 * HEAD_DIM]
            vt_ref[kt, :HEAD_DIM] = jnp.transpose(v_t).astype(bf16)
            vt_ref[kt, HEAD_DIM:] = jnp.ones((DENOM_ROWS, KEY_TILE), bf16)
            return carry

        lax.fori_loop(0, n_qblk * Q_BLOCK // KEY_TILE, vt_tile, 0)

    def rope_rows(x, cos_t, sin_t):
        half = x.shape[0] // 2
        x1, x2 = x[:half], x[half:]
        return jnp.concatenate([x1 * cos_t - x2 * sin_t, x1 * sin_t + x2 * cos_t], axis=0)

    cos_i, sin_i = cos_i_ref[...], sin_i_ref[...]
    for c in range(qi_ref.shape[1] // LANES):
        t = jnp.transpose(qi_ref[:, c * LANES:(c + 1) * LANES])
        for j in range(LANES // IDX_DIM):
            h = c * (LANES // IDX_DIM) + j
            qit_ref[:, h * Q_BLOCK:(h + 1) * Q_BLOCK] = rope_rows(
                t[j * IDX_DIM:(j + 1) * IDX_DIM], cos_i, sin_i).astype(bf16)
    cos_a, sin_a, qgain = cos_a_ref[...], sin_a_ref[...], qgain_ref[...]
    for h in range(heads):
        t = jnp.transpose(q_ref[:, h * HEAD_DIM:(h + 1) * HEAD_DIM])
        ms = jnp.mean(t * t, axis=0, keepdims=True)
        t = rope_rows(t * lax.rsqrt(ms + EPS) * qgain, cos_a, sin_a)
        q2t_ref[:, h * Q_BLOCK:(h + 1) * Q_BLOCK] = (t * (HEAD_DIM ** -0.5 * LOG2_E)).astype(bf16)
    w_t = jnp.transpose(ksq_ref[...])
    q_chunk = (qb * Q_BLOCK + lax.broadcasted_iota(jnp.int32, (KEY_TILE, Q_BLOCK), 1)) // CHUNK
    key_off = lax.broadcasted_iota(jnp.int32, (KEY_TILE, Q_BLOCK), 0)

    def score_group(g, carry):
        rows = SCORE_GROUP * KEY_TILE
        base = pl.multiple_of(g * rows, rows)
        ki = ks_ref[pl.ds(base, rows), :][:, 2 * HEAD_DIM:2 * HEAD_DIM + IDX_DIM].astype(bf16)
        rel = jnp.dot(ki, qit_ref[...], preferred_element_type=f32)
        for u in range(SCORE_GROUP):
            us = slice(u * KEY_TILE, (u + 1) * KEY_TILE)
            s = jnp.zeros((KEY_TILE, Q_BLOCK), f32)
            for h in range(idx_heads):
                r = jnp.maximum(rel[us, h * Q_BLOCK:(h + 1) * Q_BLOCK], 0.0)
                s = s + w_t[IDX_DIM + h:IDX_DIM + h + 1, :] * r
            adm = (base + u * KEY_TILE + key_off) // CHUNK <= q_chunk
            keys_ref[g * SCORE_GROUP + u] = jnp.where(adm, _sortable(s), INT_MIN)
        return carry

    lax.fori_loop(0, n_grp, score_group, 0)

    def count_keys(pred):
        def count_group(g, cnt):
            for u in range(SCORE_GROUP):
                kt = g * SCORE_GROUP + u
                hit = pred(keys_ref[kt], kt * KEY_TILE + key_off).astype(jnp.int32)
                cnt = cnt + jnp.sum(hit.reshape(KEY_TILE // 8, 8, Q_BLOCK), axis=0)
            return cnt

        cnt = lax.fori_loop(0, n_grp, count_group, jnp.zeros((8, Q_BLOCK), jnp.int32))
        return jnp.sum(cnt, axis=0, keepdims=True)

    def bit_step(i, carry):
        prefix, n_ge = carry
        cand = prefix + jnp.left_shift(jnp.int32(1), 31 - i)
        total = count_keys(lambda kk, idx: kk >= cand)
        take = total >= top
        return jnp.where(take, cand, prefix), jnp.where(take, total, n_ge)

    thr, n_ge = lax.fori_loop(
        0, 32, bit_step,
        (jnp.full((1, Q_BLOCK), INT_MIN, jnp.int32), jnp.zeros((1, Q_BLOCK), jnp.int32)))

    def tie_search(_):
        def idx_step(i, bound):
            cand = bound + jnp.left_shift(jnp.int32(1), idx_bits - 1 - i)
            total = count_keys(lambda kk, idx: (kk > thr) | ((kk == thr) & (idx < cand)))
            return jnp.where(total < top, cand, bound)

        return lax.fori_loop(0, idx_bits, idx_step, jnp.zeros((1, Q_BLOCK), jnp.int32))

    idx_bits = int(keys_ref.shape[0] * KEY_TILE).bit_length()
    no_ties = lambda _: jnp.full((1, Q_BLOCK), keys_ref.shape[0] * KEY_TILE, jnp.int32)
    last_idx = lax.cond(jnp.max(n_ge) > top, tie_search, no_ties, 0)

    m_ref[...] = jnp.full(m_ref.shape, NEG_BIG, f32)
    acc_ref[...] = jnp.zeros(acc_ref.shape, f32)

    def attn_group(g, carry):
        rows = ATTN_GROUP * KEY_TILE
        base = pl.multiple_of(g * rows, rows)
        k_t = ks_ref[pl.ds(base, rows), :][:, :HEAD_DIM].astype(bf16)
        sel = []
        for u in range(ATTN_GROUP):
            kk = keys_ref[g * ATTN_GROUP + u]
            idx = base + u * KEY_TILE + key_off
            sel.append(((kk > thr) | ((kk == thr) & (idx <= last_idx))) & (kk != INT_MIN))
        bias_ref[...] = jnp.where(jnp.concatenate(sel, axis=0), 0.0, NEG_BIG)
        v_t = jnp.concatenate([vt_ref[g * ATTN_GROUP + u] for u in range(ATTN_GROUP)], axis=1)
        logits = jnp.dot(k_t, q2t_ref[...], preferred_element_type=f32)
        probs, alphas = [], []
        for h in range(heads):
            hs = slice(h * Q_BLOCK, (h + 1) * Q_BLOCK)
            lg = logits[:, hs] + bias_ref[...]
            m_old = m_ref[:, hs]
            m_new = jnp.maximum(m_old, jnp.max(lg, axis=0, keepdims=True))
            m_ref[:, hs] = m_new
            alphas.append(jnp.exp2(m_old - m_new))
            probs.append(jnp.exp2(lg - m_new).astype(bf16))
        pv = jnp.dot(v_t, jnp.concatenate(probs, axis=1),
                     preferred_element_type=f32)
        acc_ref[...] = acc_ref[...] * jnp.concatenate(alphas, axis=1) + pv
        return carry

    lax.fori_loop(0, n_grp_attn, attn_group, 0)

    out = acc_ref[:HEAD_DIM] / acc_ref[HEAD_DIM:HEAD_DIM + 1]
    for h in range(heads):
        o_ref[:, h * HEAD_DIM:(h + 1) * HEAD_DIM] = jnp.transpose(
            out[:, h * Q_BLOCK:(h + 1) * Q_BLOCK]).astype(o_ref.dtype)


def dsa_attention(q, qi, keyside, q_gain, rope_a, rope_i, batch, seq):
    n, aw = q.shape
    n_qblk = seq // Q_BLOCK
    n_kt = seq // KEY_TILE
    assert n_kt % SCORE_GROUP == 0 and SCORE_GROUP % ATTN_GROUP == 0
    heads = aw // HEAD_DIM
    iw = qi.shape[1]
    top = min(TOPK_MAX, seq // 4)
    q_gain_t = jnp.broadcast_to(q_gain.reshape(HEAD_DIM, 1), (HEAD_DIM, Q_BLOCK))
    per_qblk = lambda rows: pl.BlockSpec((rows, Q_BLOCK), lambda g: (0, g % n_qblk))
    return pl.pallas_call(
        functools.partial(_dsa_kernel, n_qblk=n_qblk, top=top),
        grid=(batch * n_qblk,),
        in_specs=[pl.BlockSpec((Q_BLOCK, aw), lambda g: (g, 0)),
                  pl.BlockSpec((Q_BLOCK, iw), lambda g: (g, 0)),
                  pl.BlockSpec((seq, keyside.shape[1]), lambda g: (g // n_qblk, 0)),
                  pl.BlockSpec((Q_BLOCK, LANES), lambda g: (g, 2 * HEAD_DIM // LANES)),
                  pl.BlockSpec((HEAD_DIM, Q_BLOCK), lambda g: (0, 0)),
                  per_qblk(HEAD_DIM // 2), per_qblk(HEAD_DIM // 2),
                  per_qblk(IDX_DIM // 2), per_qblk(IDX_DIM // 2)],
        out_specs=pl.BlockSpec((Q_BLOCK, aw), lambda g: (g, 0)),
        out_shape=jax.ShapeDtypeStruct((n, aw), bf16),
        scratch_shapes=[pltpu.VMEM((IDX_DIM, (iw // IDX_DIM) * Q_BLOCK), bf16),
                        pltpu.VMEM((HEAD_DIM, heads * Q_BLOCK), bf16),
                        pltpu.VMEM((n_kt, HEAD_DIM + DENOM_ROWS, KEY_TILE), bf16),
                        pltpu.VMEM((n_kt, KEY_TILE, Q_BLOCK), jnp.int32),
                        pltpu.VMEM((ATTN_GROUP * KEY_TILE, Q_BLOCK), f32),
                        pltpu.VMEM((1, heads * Q_BLOCK), f32),
                        pltpu.VMEM((HEAD_DIM + DENOM_ROWS, heads * Q_BLOCK), f32)],
        compiler_params=_params(("arbitrary",)),
        name="dsa_attention",
    )(q, qi, keyside, keyside, q_gain_t, *rope_a, *rope_i)


def _merge_kernel(ya_ref, yb_ref, wa_ref, wb_ref, ga_ref, gb_ref, o_ref):
    a = jnp.dot(ya_ref[...], wa_ref[...], preferred_element_type=f32)
    b = jnp.dot(yb_ref[...], wb_ref[...], preferred_element_type=f32)
    o_ref[...] = (ga_ref[...].astype(f32) * a + gb_ref[...].astype(f32) * b).astype(o_ref.dtype)


def gated_merge(ya, yb, wa, wb, gates, tm=1024, tn=512):
    n, ka = ya.shape
    kb = yb.shape[1]
    d = wa.shape[1]
    tm, tn = _tile(n, tm), _tile(d, tn)
    nj = d // tn
    assert d % tn == 0
    return pl.pallas_call(
        _merge_kernel,
        grid=(pl.cdiv(n, tm), nj),
        in_specs=[pl.BlockSpec((tm, ka), lambda i, j: (i, 0)),
                  pl.BlockSpec((tm, kb), lambda i, j: (i, 0)),
                  pl.BlockSpec((ka, tn), lambda i, j: (0, j)),
                  pl.BlockSpec((kb, tn), lambda i, j: (0, j)),
                  pl.BlockSpec((tm, tn), lambda i, j: (i, j)),
                  pl.BlockSpec((tm, tn), lambda i, j: (i, j + nj))],
        out_specs=pl.BlockSpec((tm, tn), lambda i, j: (i, j)),
        out_shape=jax.ShapeDtypeStruct((n, d), bf16),
        compiler_params=_params(("parallel", "arbitrary")),
        name="gated_merge",
    )(ya, yb, wa, wb, gates, gates)


def _rope_tables(seq, dim, lanes):
    inv = ROPE_THETA ** (-jnp.arange(0, dim, 2, dtype=f32) / dim)
    ang = jnp.arange(seq, dtype=f32)[:, None] * inv[None, :]
    cos, sin = jnp.cos(ang), jnp.sin(ang)
    reps = lanes // dim
    cos_t = jnp.tile(jnp.concatenate([cos, cos], axis=-1), (1, reps))
    sin_t = jnp.tile(jnp.concatenate([-sin, sin], axis=-1), (1, reps))
    return cos_t, sin_t


def _rope_tables_t(seq, dim):
    inv = ROPE_THETA ** (-jnp.arange(0, dim, 2, dtype=f32) / dim)
    ang = inv[:, None] * jnp.arange(seq, dtype=f32)[None, :]
    return jnp.cos(ang), jnp.sin(ang)


def _fold_gain(gain, w):
    return (gain[:, None] * w).astype(bf16)


def _project(h16, h_ss, w16_t, col_off, n_cols, epilogue, out_dtype, tn, **kw):
    tn = min(tn, n_cols)
    if col_off % tn == 0 and n_cols % tn == 0:
        return matmul(h16, w16_t, epilogue, out_dtype, tn=tn, n_cols=n_cols, col_off=col_off,
                      row_sumsq=h_ss, w_rows=True, **kw)
    return matmul(h16, w16_t[col_off:col_off + n_cols], epilogue, out_dtype, tn=tn,
                  row_sumsq=h_ss, w_rows=True, **kw)


def kernel(x, ffn1_norm, ffn1_w1, ffn1_w3, ffn1_w2, mix_norm, w_in, gate_bias,
           gmlp_v_norm, gmlp_ws, gmlp_bs, q_norm, k_norm, idx_k_norm,
           w_br_a, w_br_b, w_out, ffn2_norm, ffn2_w1, ffn2_w3, ffn2_w2):
    batch, seq, d = x.shape
    depth = ffn1_norm.shape[0]
    n = batch * seq
    gw = gmlp_v_norm.shape[1]
    aw = w_br_b.shape[1]
    idx_heads = d // 128
    iw = idx_heads * IDX_DIM
    d_ff = ffn1_w1.shape[2]
    nk_down = 2 if (d_ff % (2 * LANES) == 0 and d_ff > 4096) else 1
    c_uv, c_q, c_kv, c_qi = 2 * gw, aw, 2 * HEAD_DIM, iw
    o_q = c_uv
    o_kv = o_q + c_q
    o_qi = o_kv + c_kv
    o_small = o_qi + c_qi
    o_gate = o_small + IDX_DIM + idx_heads
    assert o_gate + 2 * d == w_in.shape[2]
    assert IDX_DIM + idx_heads <= LANES
    idx_scale = (idx_heads ** -0.5) * (IDX_DIM ** -0.5)

    cos_a, sin_a = _rope_tables(seq, HEAD_DIM, LANES)
    cos_i, sin_i = _rope_tables(seq, IDX_DIM, LANES)
    rope_a_t = _rope_tables_t(seq, HEAD_DIM)
    rope_i_t = _rope_tables_t(seq, IDX_DIM)
    tm = _tile(seq, 1024)
    assert seq % tm == 0 and seq % Q_BLOCK == 0

    h = x.reshape(n, d)
    h16, h_ss = cast_sumsq(h)
    for l in range(depth):
        w_in_t = jnp.transpose(w_in[l])
        mix_gain = mix_norm[l].reshape(1, d)
        up_steps = pl.cdiv(n, min(n, 1024)) * pl.cdiv(d_ff, min(d_ff, 512))
        split_gate = o_gate % _side_tile_rows(2 * d, up_steps) == 0
        w_in_sides = ([(w_in_t, mix_gain, 0, o_gate), (w_in_t, mix_gain, o_gate, 2 * d)]
                      if split_gate else [(w_in_t, mix_gain)])
        g, (w2_16, wa16, wb16, wo16, *w_in16) = swiglu_up(
            h16, h_ss, _fold_gain(ffn1_norm[l], ffn1_w1[l]), _fold_gain(ffn1_norm[l], ffn1_w3[l]),
            sides=[(ffn1_w2[l], None), (w_br_a[l], None), (w_br_b[l], None), (w_out[l], None)]
            + w_in_sides)
        w_in16_t = w_in16[0]
        w_gate16_t = w_in16[1] if split_gate else w_in16_t[o_gate:]
        h, h16, h_ss = matmul(g, w2_16, _ep_half_residual, f32, extras=[(h, "tile")],
                              nk=nk_down, emit_norm=True, name="ffn_down")

        ffn2_gain = ffn2_norm[l].reshape(d, 1)
        uv = _project(h16, h_ss, w_in16_t, 0, c_uv, _ep_gelu, bf16, WIDE_TN, tm=tm,
                      name="proj_uv")
        q = _project(h16, h_ss, w_in16_t, o_q, c_q, _ep_identity, f32, WIDE_TN, tm=tm,
                     name="proj_q")
        qi = _project(h16, h_ss, w_in16_t, o_qi, c_qi, _ep_identity, f32, WIDE_TN, tm=tm,
                      name="proj_qi")
        w_keyside = jnp.concatenate(
            [w_in16_t[o_kv:o_kv + c_kv],
             jnp.pad(w_in16_t[o_small:o_gate], ((0, LANES - (o_gate - o_small)), (0, 0)))], axis=0)
        ki_gain = jnp.tile(idx_k_norm[l].reshape(1, IDX_DIM), (1, LANES // IDX_DIM))
        keyside = matmul(h16, w_keyside, functools.partial(_ep_keyside, idx_scale=idx_scale), f32,
                         tm=tm, tn=c_kv + LANES, row_sumsq=h_ss, w_rows=True,
                         extras=[(k_norm[l].reshape(1, HEAD_DIM), "const"), (cos_a, "pos"),
                                 (sin_a, "pos"), (ki_gain, "const"), (cos_i, "pos"),
                                 (sin_i, "pos")], name="proj_keyside")
        gates, w1b_16, w3b_16 = _project(
            h16, h_ss, w_gate16_t, 0, 2 * d, _ep_gate, bf16, WIDE_TN, tm=tm,
            extras=[(gate_bias[l].reshape(1, 2 * d), "row")],
            sides=[(ffn2_w1[l], ffn2_gain), (ffn2_w3[l], ffn2_gain)], name="proj_gate")

        ya = gmlp_gate(uv, gmlp_v_norm[l], gmlp_ws[l], gmlp_bs[l].T)
        yb = dsa_attention(q, qi, keyside, q_norm[l], rope_a_t, rope_i_t, batch, seq)
        m = gated_merge(ya, yb, wa16, wb16, gates, tm=tm, tn=WIDE_TN)
        h, h16, h_ss = matmul(m, wo16, _ep_residual, f32, extras=[(h, "tile")],
                              tm=tm, emit_norm=True, name="out_proj")

        g, (w2b_16,) = swiglu_up(h16, h_ss, w1b_16, w3b_16, sides=[(ffn2_w2[l], None)])
        last = l == depth - 1
        out = matmul(g, w2b_16, _ep_half_residual, f32, extras=[(h, "tile")], nk=nk_down,
                     emit_norm=not last, name="ffn_down")
        h, h16, h_ss = (out, None, None) if last else out
    return h.reshape(batch, seq, d)
```

```python
import functools
import math

import jax
import jax.numpy as jnp
from jax import lax
from jax.experimental import pallas as pl
from jax.experimental.pallas import tpu as pltpu

CHUNK = 64
GROUP_DIM = 128
MIX_BLOCK = 128
GMLP_BLOCKS_PER_STEP = 4
HEAD_DIM = 128
IDX_DIM = 64
Q_BLOCK = 256
KEY_TILE = 128
SCORE_GROUP = 2
ATTN_GROUP = 2
DENOM_ROWS = 16
LOG2_E = math.log2(math.e)
TOPK_MAX = 256
ROPE_THETA = 10000.0
EPS = 1e-6
LANES = 128
BF16_SUBLANES = 16
NEG_BIG = -1e30
INT_MIN = -(2 ** 31)
VMEM_LIMIT_BYTES = 60000 * 1024
WIDE_TN = 1024

f32 = jnp.float32
bf16 = jnp.bfloat16


def _params(sem):
    return pltpu.CompilerParams(dimension_semantics=sem,
                                vmem_limit_bytes=VMEM_LIMIT_BYTES)


def _tile(dim, want):
    return dim if dim <= want else want


def _sigmoid(x):
    return 0.5 * jnp.tanh(0.5 * x) + 0.5


def _gelu(x):
    return 0.5 * x * (1.0 + lax.erf(x * (2.0 ** -0.5)))


def _rope_pairs(x, cos_t, sin_t, half):
    width = x.shape[-1]
    lane = lax.broadcasted_iota(jnp.int32, x.shape, x.ndim - 1)
    fwd = pltpu.roll(x, width - half, axis=x.ndim - 1)
    bwd = pltpu.roll(x, half, axis=x.ndim - 1)
    partner = jnp.where((lane % (2 * half)) < half, fwd, bwd)
    return x * cos_t + partner * sin_t


def _row_rsqrt(sumsq, width):
    return lax.rsqrt(sumsq * (1.0 / width) + EPS)


def _cast_sumsq_kernel(x_ref, o_ref, ss_ref):
    x = x_ref[...]
    o_ref[...] = x.astype(o_ref.dtype)
    ss_ref[...] = jnp.sum(x * x, axis=-1, keepdims=True)


def cast_sumsq(x, tm=256):
    n, d = x.shape
    tm = _tile(n, tm)
    return pl.pallas_call(
        _cast_sumsq_kernel,
        grid=(pl.cdiv(n, tm),),
        in_specs=[pl.BlockSpec((tm, d), lambda i: (i, 0))],
        out_specs=[pl.BlockSpec((tm, d), lambda i: (i, 0)),
                   pl.BlockSpec((tm, 1), lambda i: (i, 0))],
        out_shape=[jax.ShapeDtypeStruct((n, d), bf16),
                   jax.ShapeDtypeStruct((n, 1), f32)],
        compiler_params=_params(("parallel",)),
        name="cast_sumsq",
    )(x)


def _side_tile_rows(rows, steps):
    return BF16_SUBLANES * (-(-rows // (BF16_SUBLANES * steps)))


def _side_plan(sides, steps, flat_step):
    in_specs, in_arrays, out_specs, out_shapes, scaled = [], [], [], [], []
    for src, scale, *row_range in sides:
        first_row, rows = row_range if row_range else (0, src.shape[0])
        cols = src.shape[1]
        tr = _side_tile_rows(rows, steps)
        n_tiles = -(-rows // tr)
        assert n_tiles <= steps and first_row % tr == 0
        idx = lambda *g, n_tiles=n_tiles: (jnp.minimum(flat_step(*g), n_tiles - 1), 0)
        src_idx = lambda *g, idx=idx, off=first_row // tr: (idx(*g)[0] + off, 0)
        in_specs.append(pl.BlockSpec((tr, cols), src_idx))
        in_arrays.append(src)
        if scale is not None:
            if scale.shape == (1, cols):
                in_specs.append(pl.BlockSpec((1, cols), lambda *g: (0, 0)))
            else:
                assert scale.shape == (rows, 1) and not row_range
                in_specs.append(pl.BlockSpec((tr, 1), idx))
            in_arrays.append(scale)
        scaled.append(scale is not None)
        out_specs.append(pl.BlockSpec((tr, cols), idx))
        out_shapes.append(jax.ShapeDtypeStruct((rows, cols), bf16))
    return in_specs, in_arrays, out_specs, out_shapes, tuple(scaled)


def _take_side_inputs(it, scaled):
    return [(next(it), next(it) if s else None) for s in scaled]


def _run_sides(side_in, side_out):
    for (src_ref, scale_ref), dst_ref in zip(side_in, side_out):
        v = src_ref[...]
        if scale_ref is not None:
            v = v * scale_ref[...]
        dst_ref[...] = v.astype(dst_ref.dtype)


def _swiglu_up_kernel(*refs, scaled):
    it = iter(refs)
    x_ref, ss_ref, w1_ref, w3_ref = next(it), next(it), next(it), next(it)
    side_in = _take_side_inputs(it, scaled)
    o_ref = next(it)
    side_out = [next(it) for _ in scaled]
    x = x_ref[...]
    r = _row_rsqrt(ss_ref[...], x.shape[1])
    a = jnp.dot(x, w1_ref[...], preferred_element_type=f32) * r
    b = jnp.dot(x, w3_ref[...], preferred_element_type=f32) * r
    o_ref[...] = (a * _sigmoid(a) * b).astype(o_ref.dtype)
    _run_sides(side_in, side_out)


def swiglu_up(x16, sumsq, w1, w3, sides=(), tm=1024, tn=512):
    n, d = x16.shape
    f = w1.shape[1]
    tm, tn = _tile(n, tm), _tile(f, tn)
    ni, nj = pl.cdiv(n, tm), pl.cdiv(f, tn)
    s_in, s_arr, s_out, s_shape, scaled = _side_plan(sides, ni * nj, lambda i, j: i * nj + j)
    outs = pl.pallas_call(
        functools.partial(_swiglu_up_kernel, scaled=scaled),
        grid=(ni, nj),
        in_specs=[pl.BlockSpec((tm, d), lambda i, j: (i, 0)),
                  pl.BlockSpec((tm, 1), lambda i, j: (i, 0)),
                  pl.BlockSpec((d, tn), lambda i, j: (0, j)),
                  pl.BlockSpec((d, tn), lambda i, j: (0, j))] + s_in,
        out_specs=[pl.BlockSpec((tm, tn), lambda i, j: (i, j))] + s_out,
        out_shape=[jax.ShapeDtypeStruct((n, f), bf16)] + s_shape,
        compiler_params=_params(("arbitrary", "arbitrary")),
        name="swiglu_up",
    )(x16, sumsq, w1, w3, *s_arr)
    return outs[0], outs[1:]


def _mm_kernel(*refs, nk, kdim, n_extra, epilogue, row_norm, emit_norm, scaled, w_rows):
    it = iter(refs)
    a_ref, w_ref = next(it), next(it)
    ss_in_ref = next(it) if row_norm else None
    extra = [next(it) for _ in range(n_extra)]
    side_in = _take_side_inputs(it, scaled)
    o_ref = next(it)
    o16_ref, ss_out_ref = (next(it), next(it)) if emit_norm else (None, None)
    side_out = [next(it) for _ in scaled]
    acc_ref = next(it) if nk > 1 else None
    j, k = pl.program_id(1), pl.program_id(2)

    if nk > 1:
        @pl.when(k == 0)
        def _():
            acc_ref[...] = jnp.zeros(acc_ref.shape, f32)

    if emit_norm:
        @pl.when((j == 0) & (k == 0))
        def _():
            ss_out_ref[...] = jnp.zeros(ss_out_ref.shape, f32)

    dims = (((1,), (1,)), ((), ())) if w_rows else (((1,), (0,)), ((), ()))
    acc = lax.dot_general(a_ref[...], w_ref[...], dims, preferred_element_type=f32)
    if nk > 1:
        acc = acc_ref[...] + acc
        acc_ref[...] = acc
    if row_norm:
        acc = acc * _row_rsqrt(ss_in_ref[...], kdim)
    out = epilogue(acc, *[e[...] for e in extra])
    o_ref[...] = out.astype(o_ref.dtype)
    if emit_norm:
        o16_ref[...] = out.astype(o16_ref.dtype)
        part_ss = jnp.sum(out * out, axis=-1, keepdims=True)
        ss_out_ref[...] += jnp.where(k == nk - 1, part_ss, 0.0)

    _run_sides(side_in, side_out)


def matmul(a, w, epilogue, out_dtype, extras=(), tm=1024, tn=512, nk=1, name="mm",
           n_cols=None, col_off=0, row_sumsq=None, emit_norm=False, sides=(), w_rows=False):
    m, kdim = a.shape
    n = w.shape[0 if w_rows else 1] if n_cols is None else n_cols
    tm, tn = _tile(m, tm), _tile(n, tn)
    assert kdim % nk == 0 and col_off % tn == 0
    tk = kdim // nk
    assert nk == 1 or tk % LANES == 0
    ni, nj, joff = pl.cdiv(m, tm), pl.cdiv(n, tn), col_off // tn
    in_specs = [pl.BlockSpec((tm, tk), lambda i, j, k: (i, k)),
                pl.BlockSpec((tn, tk), lambda i, j, k: (j + joff, k)) if w_rows else
                pl.BlockSpec((tk, tn), lambda i, j, k: (k, j + joff))]
    arrays = []
    if row_sumsq is not None:
        in_specs.append(pl.BlockSpec((tm, 1), lambda i, j, k: (i, 0)))
        arrays.append(row_sumsq)
    n_extra = len(extras)
    for arr, kind in extras:
        arrays.append(arr)
        if kind == "tile":
            in_specs.append(pl.BlockSpec((tm, tn), lambda i, j, k: (i, j)))
        elif kind == "row":
            in_specs.append(pl.BlockSpec((1, tn), lambda i, j, k: (0, j)))
        elif kind == "const":
            in_specs.append(pl.BlockSpec(arr.shape, lambda i, j, k: (0, 0)))
        elif kind == "pos":
            assert arr.shape[0] % tm == 0
            nper = arr.shape[0] // tm
            in_specs.append(pl.BlockSpec((tm, arr.shape[1]),
                                         lambda i, j, k, nper=nper: (i % nper, 0)))
        else:
            raise ValueError(kind)
    s_in, s_arr, s_out, s_shape, scaled = _side_plan(
        sides, ni * nj * nk, lambda i, j, k: (i * nj + j) * nk + k)
    out_specs = [pl.BlockSpec((tm, tn), lambda i, j, k: (i, j))]
    out_shape = [jax.ShapeDtypeStruct((m, n), out_dtype)]
    if emit_norm:
        out_specs += [pl.BlockSpec((tm, tn), lambda i, j, k: (i, j)),
                      pl.BlockSpec((tm, 1), lambda i, j, k: (i, 0))]
        out_shape += [jax.ShapeDtypeStruct((m, n), bf16), jax.ShapeDtypeStruct((m, 1), f32)]
    scratch = [pltpu.VMEM((tm, tn), f32)] if nk > 1 else []
    outs = pl.pallas_call(
        functools.partial(_mm_kernel, nk=nk, kdim=kdim, n_extra=n_extra, epilogue=epilogue,
                          row_norm=row_sumsq is not None, emit_norm=emit_norm, scaled=scaled,
                          w_rows=w_rows),
        grid=(ni, nj, nk),
        in_specs=in_specs + s_in,
        out_specs=out_specs + s_out,
        out_shape=out_shape + s_shape,
        scratch_shapes=scratch,
        compiler_params=_params(("arbitrary", "arbitrary", "arbitrary")),
        name=name,
    )(a, w, *arrays, *s_arr)
    return outs[0] if len(outs) == 1 else tuple(outs)


def _ep_half_residual(acc, res):
    return res + 0.5 * acc


def _ep_residual(acc, res):
    return res + acc


def _ep_gelu(acc):
    return _gelu(acc)


def _ep_gate(acc, bias):
    return _sigmoid(acc + bias)


def _ep_identity(acc):
    return acc


def _ep_kv(acc, gain, cos_t, sin_t):
    k = acc[:, :HEAD_DIM]
    ms = jnp.mean(k * k, axis=-1, keepdims=True)
    k = _rope_pairs(k * lax.rsqrt(ms + EPS) * gain, cos_t, sin_t, HEAD_DIM // 2)
    return jnp.concatenate([k, acc[:, HEAD_DIM:]], axis=-1)


def _ep_small(acc, gain, cos_t, sin_t, *, idx_scale):
    lane = lax.broadcasted_iota(jnp.int32, acc.shape, 1)
    is_k = lane < IDX_DIM
    ms = jnp.sum(jnp.where(is_k, acc * acc, 0.0), axis=-1, keepdims=True) * (1.0 / IDX_DIM)
    k = _rope_pairs(acc * lax.rsqrt(ms + EPS) * gain, cos_t, sin_t, IDX_DIM // 2)
    return jnp.where(is_k, k, acc * idx_scale)


def _ep_keyside(acc, k_gain, cos_a, sin_a, ki_gain, cos_i, sin_i, *, idx_scale):
    kv = _ep_kv(acc[:, :2 * HEAD_DIM], k_gain, cos_a, sin_a)
    small = _ep_small(acc[:, 2 * HEAD_DIM:], ki_gain, cos_i, sin_i, idx_scale=idx_scale)
    return jnp.concatenate([kv, small], axis=-1)


def _gmlp_kernel(u_ref, v_ref, gain_ref, ws_ref, bs_ref, o_ref):
    groups = ws_ref.shape[0]
    v = v_ref[...].astype(f32)
    ms = jnp.mean(v * v, axis=-1, keepdims=True)
    vn = (v * lax.rsqrt(ms + EPS) * gain_ref[...]).astype(bf16)
    row_chunk = lax.broadcasted_iota(jnp.int32, (MIX_BLOCK, MIX_BLOCK), 0) // CHUNK
    col_chunk = lax.broadcasted_iota(jnp.int32, (MIX_BLOCK, MIX_BLOCK), 1) // CHUNK
    causal = col_chunk <= row_chunk
    bs = bs_ref[...]
    for g in range(groups):
        sl = slice(g * GROUP_DIM, (g + 1) * GROUP_DIM)
        w = jnp.where(causal, ws_ref[g], 0.0).astype(bf16)
        for b in range(v_ref.shape[0] // MIX_BLOCK):
            rows = slice(b * MIX_BLOCK, (b + 1) * MIX_BLOCK)
            mixed = jnp.dot(w, vn[rows, sl], preferred_element_type=f32) + bs[:, g:g + 1]
            o_ref[rows, sl] = (u_ref[rows, sl].astype(f32) * mixed).astype(o_ref.dtype)


def gmlp_gate(uv, gain, ws, bs_t):
    n = uv.shape[0]
    gw = uv.shape[1] // 2
    groups = gw // GROUP_DIM
    rows = MIX_BLOCK * GMLP_BLOCKS_PER_STEP
    assert n % rows == 0
    return pl.pallas_call(
        _gmlp_kernel,
        grid=(n // rows,),
        in_specs=[pl.BlockSpec((rows, gw), lambda i: (i, 0)),
                  pl.BlockSpec((rows, gw), lambda i: (i, 1)),
                  pl.BlockSpec((1, gw), lambda i: (0, 0)),
                  pl.BlockSpec((groups, MIX_BLOCK, MIX_BLOCK), lambda i: (0, 0, 0)),
                  pl.BlockSpec((MIX_BLOCK, groups), lambda i: (0, 0))],
        out_specs=pl.BlockSpec((rows, gw), lambda i: (i, 0)),
        out_shape=jax.ShapeDtypeStruct((n, gw), bf16),
        compiler_params=_params(("parallel",)),
        name="gmlp_gate",
    )(uv, uv, gain.reshape(1, gw), ws, bs_t)


def _sortable(x):
    bits = lax.bitcast_convert_type(x, jnp.int32)
    return bits ^ ((bits >> 31) & jnp.int32(0x7FFFFFFF))


def _dsa_kernel(q_ref, qi_ref, ks_ref, ksq_ref, qgain_ref,
                cos_a_ref, sin_a_ref, cos_i_ref, sin_i_ref, o_ref,
                qit_ref, q2t_ref, vt_ref, keys_ref, bias_ref, m_ref, acc_ref,
                *, n_qblk, top):
    qb = pl.program_id(0) % n_qblk
    n_kt = (qb + 1) * (Q_BLOCK // KEY_TILE)
    n_grp = (n_kt + SCORE_GROUP - 1) // SCORE_GROUP
    n_grp_attn = (n_kt + ATTN_GROUP - 1) // ATTN_GROUP
    heads = q_ref.shape[1] // HEAD_DIM
    idx_heads = qi_ref.shape[1] // IDX_DIM

    @pl.when(qb == 0)
    def _():
        def vt_tile(kt, carry):
            base = pl.multiple_of(kt * KEY_TILE, KEY_TILE)
            v_t = ks_ref[pl.ds(base, KEY_TILE), :][:, HEAD_DIM:2 * HEAD_DIM]
            vt_ref[kt, :HEAD_DIM] = jnp.transpose(v_t).astype(bf16)
            vt_ref[kt, HEAD_DIM:] = jnp.ones((DENOM_ROWS, KEY_TILE), bf16)
            return carry

        lax.fori_loop(0, n_qblk * Q_BLOCK // KEY_TILE, vt_tile, 0)

    def rope_rows(x, cos_t, sin_t):
        half = x.shape[0] // 2
        x1, x2 = x[:half], x[half:]
        return jnp.concatenate([x1 * cos_t - x2 * sin_t, x1 * sin_t + x2 * cos_t], axis=0)

    cos_i, sin_i = cos_i_ref[...], sin_i_ref[...]
    for c in range(qi_ref.shape[1] // LANES):
        t = jnp.transpose(qi_ref[:, c * LANES:(c + 1) * LANES])
        for j in range(LANES // IDX_DIM):
            h = c * (LANES // IDX_DIM) + j
            qit_ref[:, h * Q_BLOCK:(h + 1) * Q_BLOCK] = rope_rows(
                t[j * IDX_DIM:(j + 1) * IDX_DIM], cos_i, sin_i).astype(bf16)
    cos_a, sin_a, qgain = cos_a_ref[...], sin_a_ref[...], qgain_ref[...]
    for h in range(heads):
        t = jnp.transpose(q_ref[:, h * HEAD_DIM:(h + 1) * HEAD_DIM])
        ms = jnp.mean(t * t, axis=0, keepdims=True)
        t = rope_rows(t * lax.rsqrt(ms + EPS) * qgain, cos_a, sin_a)
        q2t_ref[:, h * Q_BLOCK:(h + 1) * Q_BLOCK] = (t * (HEAD_DIM ** -0.5 * LOG2_E)).astype(bf16)
    w_t = jnp.transpose(ksq_ref[...])
    q_chunk = (qb * Q_BLOCK + lax.broadcasted_iota(jnp.int32, (KEY_TILE, Q_BLOCK), 1)) // CHUNK
    key_off = lax.broadcasted_iota(jnp.int32, (KEY_TILE, Q_BLOCK), 0)

    def score_group(g, carry):
        rows = SCORE_GROUP * KEY_TILE
        base = pl.multiple_of(g * rows, rows)
        ki = ks_ref[pl.ds(base, rows), :][:, 2 * HEAD_DIM:2 * HEAD_DIM + IDX_DIM].astype(bf16)
        rel = jnp.dot(ki, qit_ref[...], preferred_element_type=f32)
        for u in range(SCORE_GROUP):
            us = slice(u * KEY_TILE, (u + 1) * KEY_TILE)
            s = jnp.zeros((KEY_TILE, Q_BLOCK), f32)
            for h in range(idx_heads):
                r = jnp.maximum(rel[us, h * Q_BLOCK:(h + 1) * Q_BLOCK], 0.0)
                s = s + w_t[IDX_DIM + h:IDX_DIM + h + 1, :] * r
            adm = (base + u * KEY_TILE + key_off) // CHUNK <= q_chunk
            keys_ref[g * SCORE_GROUP + u] = jnp.where(adm, _sortable(s), INT_MIN)
        return carry

    lax.fori_loop(0, n_grp, score_group, 0)

    def count_keys(pred):
        def count_group(g, cnt):
            for u in range(SCORE_GROUP):
                kt = g * SCORE_GROUP + u
                hit = pred(keys_ref[kt], kt * KEY_TILE + key_off).astype(jnp.int32)
                cnt = cnt + jnp.sum(hit.reshape(KEY_TILE // 8, 8, Q_BLOCK), axis=0)
            return cnt

        cnt = lax.fori_loop(0, n_grp, count_group, jnp.zeros((8, Q_BLOCK), jnp.int32))
        return jnp.sum(cnt, axis=0, keepdims=True)

    def bit_step(i, carry):
        prefix, n_ge = carry
        cand = prefix + jnp.left_shift(jnp.int32(1), 31 - i)
        total = count_keys(lambda kk, idx: kk >= cand)
        take = total >= top
        return jnp.where(take, cand, prefix), jnp.where(take, total, n_ge)

    thr, n_ge = lax.fori_loop(
        0, 32, bit_step,
        (jnp.full((1, Q_BLOCK), INT_MIN, jnp.int32), jnp.zeros((1, Q_BLOCK), jnp.int32)))

    def tie_search(_):
        def idx_step(i, bound):
            cand = bound + jnp.left_shift(jnp.int32(1), idx_bits - 1 - i)
            total = count_keys(lambda kk, idx: (kk > thr) | ((kk == thr) & (idx < cand)))
            return jnp.where(total < top, cand, bound)

        return lax.fori_loop(0, idx_bits, idx_step, jnp.zeros((1, Q_BLOCK), jnp.int32))

    idx_bits = int(keys_ref.shape[0] * KEY_TILE).bit_length()
    no_ties = lambda _: jnp.full((1, Q_BLOCK), keys_ref.shape[0] * KEY_TILE, jnp.int32)
    last_idx = lax.cond(jnp.max(n_ge) > top, tie_search, no_ties, 0)

    m_ref[...] = jnp.full(m_ref.shape, NEG_BIG, f32)
    acc_ref[...] = jnp.zeros(acc_ref.shape, f32)

    def attn_group(g, carry):
        rows = ATTN_GROUP * KEY_TILE
        base = pl.multiple_of(g * rows, rows)
        k_t = ks_ref[pl.ds(base, rows), :][:, :HEAD_DIM].astype(bf16)
        sel = []
        for u in range(ATTN_GROUP):
            kk = keys_ref[g * ATTN_GROUP + u]
            idx = base + u * KEY_TILE + key_off
            sel.append(((kk > thr) | ((kk == thr) & (idx <= last_idx))) & (kk != INT_MIN))
        bias_ref[...] = jnp.where(jnp.concatenate(sel, axis=0), 0.0, NEG_BIG)
        v_t = jnp.concatenate([vt_ref[g * ATTN_GROUP + u] for u in range(ATTN_GROUP)], axis=1)
        logits = jnp.dot(k_t, q2t_ref[...], preferred_element_type=f32)
        probs, alphas = [], []
        for h in range(heads):
            hs = slice(h * Q_BLOCK, (h + 1) * Q_BLOCK)
            lg = logits[:, hs] + bias_ref[...]
            m_old = m_ref[:, hs]
            m_new = jnp.maximum(m_old, jnp.max(lg, axis=0, keepdims=True))
            m_ref[:, hs] = m_new
            alphas.append(jnp.exp2(m_old - m_new))
            probs.append(jnp.exp2(lg - m_new).astype(bf16))
        pv = jnp.dot(v_t, jnp.concatenate(probs, axis=1),
                     preferred_element_type=f32)
        acc_ref[...] = acc_ref[...] * jnp.concatenate(alphas, axis=1) + pv
        return carry

    lax.fori_loop(0, n_grp_attn, attn_group, 0)

    out = acc_ref[:HEAD_DIM] / acc_ref[HEAD_DIM:HEAD_DIM + 1]
    for h in range(heads):
        o_ref[:, h * HEAD_DIM:(h + 1) * HEAD_DIM] = jnp.transpose(
            out[:, h * Q_BLOCK:(h + 1) * Q_BLOCK]).astype(o_ref.dtype)


def dsa_attention(q, qi, keyside, q_gain, rope_a, rope_i, batch, seq):
    n, aw = q.shape
    n_qblk = seq // Q_BLOCK
    n_kt = seq // KEY_TILE
    assert n_kt % SCORE_GROUP == 0 and SCORE_GROUP % ATTN_GROUP == 0
    heads = aw // HEAD_DIM
    iw = qi.shape[1]
    top = min(TOPK_MAX, seq // 4)
    q_gain_t = jnp.broadcast_to(q_gain.reshape(HEAD_DIM, 1), (HEAD_DIM, Q_BLOCK))
    per_qblk = lambda rows: pl.BlockSpec((rows, Q_BLOCK), lambda g: (0, g % n_qblk))
    return pl.pallas_call(
        functools.partial(_dsa_kernel, n_qblk=n_qblk, top=top),
        grid=(batch * n_qblk,),
        in_specs=[pl.BlockSpec((Q_BLOCK, aw), lambda g: (g, 0)),
                  pl.BlockSpec((Q_BLOCK, iw), lambda g: (g, 0)),
                  pl.BlockSpec((seq, keyside.shape[1]), lambda g: (g // n_qblk, 0)),
                  pl.BlockSpec((Q_BLOCK, LANES), lambda g: (g, 2 * HEAD_DIM // LANES)),
                  pl.BlockSpec((HEAD_DIM, Q_BLOCK), lambda g: (0, 0)),
                  per_qblk(HEAD_DIM // 2), per_qblk(HEAD_DIM // 2),
                  per_qblk(IDX_DIM // 2), per_qblk(IDX_DIM // 2)],
        out_specs=pl.BlockSpec((Q_BLOCK, aw), lambda g: (g, 0)),
        out_shape=jax.ShapeDtypeStruct((n, aw), bf16),
        scratch_shapes=[pltpu.VMEM((IDX_DIM, (iw // IDX_DIM) * Q_BLOCK), bf16),
                        pltpu.VMEM((HEAD_DIM, heads * Q_BLOCK), bf16),
                        pltpu.VMEM((n_kt, HEAD_DIM + DENOM_ROWS, KEY_TILE), bf16),
                        pltpu.VMEM((n_kt, KEY_TILE, Q_BLOCK), jnp.int32),
                        pltpu.VMEM((ATTN_GROUP * KEY_TILE, Q_BLOCK), f32),
                        pltpu.VMEM((1, heads * Q_BLOCK), f32),
                        pltpu.VMEM((HEAD_DIM + DENOM_ROWS, heads * Q_BLOCK), f32)],
        compiler_params=_params(("arbitrary",)),
        name="dsa_attention",
    )(q, qi, keyside, keyside, q_gain_t, *rope_a, *rope_i)


def _merge_kernel(ya_ref, yb_ref, wa_ref, wb_ref, ga_ref, gb_ref, o_ref):
    a = jnp.dot(ya_ref[...], wa_ref[...], preferred_element_type=f32)
    b = jnp.dot(yb_ref[...], wb_ref[...], preferred_element_type=f32)
    o_ref[...] = (ga_ref[...].astype(f32) * a + gb_ref[...].astype(f32) * b).astype(o_ref.dtype)


def gated_merge(ya, yb, wa, wb, gates, tm=1024, tn=512):
    n, ka = ya.shape
    kb = yb.shape[1]
    d = wa.shape[1]
    tm, tn = _tile(n, tm), _tile(d, tn)
    nj = d // tn
    assert d % tn == 0
    return pl.pallas_call(
        _merge_kernel,
        grid=(pl.cdiv(n, tm), nj),
        in_specs=[pl.BlockSpec((tm, ka), lambda i, j: (i, 0)),
                  pl.BlockSpec((tm, kb), lambda i, j: (i, 0)),
                  pl.BlockSpec((ka, tn), lambda i, j: (0, j)),
                  pl.BlockSpec((kb, tn), lambda i, j: (0, j)),
                  pl.BlockSpec((tm, tn), lambda i, j: (i, j)),
                  pl.BlockSpec((tm, tn), lambda i, j: (i, j + nj))],
        out_specs=pl.BlockSpec((tm, tn), lambda i, j: (i, j)),
        out_shape=jax.ShapeDtypeStruct((n, d), bf16),
        compiler_params=_params(("parallel", "arbitrary")),
        name="gated_merge",
    )(ya, yb, wa, wb, gates, gates)


def _rope_tables(seq, dim, lanes):
    inv = ROPE_THETA ** (-jnp.arange(0, dim, 2, dtype=f32) / dim)
    ang = jnp.arange(seq, dtype=f32)[:, None] * inv[None, :]
    cos, sin = jnp.cos(ang), jnp.sin(ang)
    reps = lanes // dim
    cos_t = jnp.tile(jnp.concatenate([cos, cos], axis=-1), (1, reps))
    sin_t = jnp.tile(jnp.concatenate([-sin, sin], axis=-1), (1, reps))
    return cos_t, sin_t


def _rope_tables_t(seq, dim):
    inv = ROPE_THETA ** (-jnp.arange(0, dim, 2, dtype=f32) / dim)
    ang = inv[:, None] * jnp.arange(seq, dtype=f32)[None, :]
    return jnp.cos(ang), jnp.sin(ang)


def _fold_gain(gain, w):
    return (gain[:, None] * w).astype(bf16)


def _project(h16, h_ss, w16_t, col_off, n_cols, epilogue, out_dtype, tn, **kw):
    tn = min(tn, n_cols)
    if col_off % tn == 0 and n_cols % tn == 0:
        return matmul(h16, w16_t, epilogue, out_dtype, tn=tn, n_cols=n_cols, col_off=col_off,
                      row_sumsq=h_ss, w_rows=True, **kw)
    return matmul(h16, w16_t[col_off:col_off + n_cols], epilogue, out_dtype, tn=tn,
                  row_sumsq=h_ss, w_rows=True, **kw)


def kernel(x, ffn1_norm, ffn1_w1, ffn1_w3, ffn1_w2, mix_norm, w_in, gate_bias,
           gmlp_v_norm, gmlp_ws, gmlp_bs, q_norm, k_norm, idx_k_norm,
           w_br_a, w_br_b, w_out, ffn2_norm, ffn2_w1, ffn2_w3, ffn2_w2):
    batch, seq, d = x.shape
    depth = ffn1_norm.shape[0]
    n = batch * seq
    gw = gmlp_v_norm.shape[1]
    aw = w_br_b.shape[1]
    idx_heads = d // 128
    iw = idx_heads * IDX_DIM
    d_ff = ffn1_w1.shape[2]
    nk_down = 2 if (d_ff % (2 * LANES) == 0 and d_ff > 4096) else 1
    c_uv, c_q, c_kv, c_qi = 2 * gw, aw, 2 * HEAD_DIM, iw
    o_q = c_uv
    o_kv = o_q + c_q
    o_qi = o_kv + c_kv
    o_small = o_qi + c_qi
    o_gate = o_small + IDX_DIM + idx_heads
    assert o_gate + 2 * d == w_in.shape[2]
    assert IDX_DIM + idx_heads <= LANES
    idx_scale = (idx_heads ** -0.5) * (IDX_DIM ** -0.5)

    cos_a, sin_a = _rope_tables(seq, HEAD_DIM, LANES)
    cos_i, sin_i = _rope_tables(seq, IDX_DIM, LANES)
    rope_a_t = _rope_tables_t(seq, HEAD_DIM)
    rope_i_t = _rope_tables_t(seq, IDX_DIM)
    tm = _tile(seq, 1024)
    assert seq % tm == 0 and seq % Q_BLOCK == 0

    h = x.reshape(n, d)
    h16, h_ss = cast_sumsq(h)
    for l in range(depth):
        w_in_t = jnp.transpose(w_in[l])
        mix_gain = mix_norm[l].reshape(1, d)
        up_steps = pl.cdiv(n, min(n, 1024)) * pl.cdiv(d_ff, min(d_ff, 512))
        split_gate = o_gate % _side_tile_rows(2 * d, up_steps) == 0
        w_in_sides = ([(w_in_t, mix_gain, 0, o_gate), (w_in_t, mix_gain, o_gate, 2 * d)]
                      if split_gate else [(w_in_t, mix_gain)])
        g, (w2_16, wa16, wb16, wo16, *w_in16) = swiglu_up(
            h16, h_ss, _fold_gain(ffn1_norm[l], ffn1_w1[l]), _fold_gain(ffn1_norm[l], ffn1_w3[l]),
            sides=[(ffn1_w2[l], None), (w_br_a[l], None), (w_br_b[l], None), (w_out[l], None)]
            + w_in_sides)
        w_in16_t = w_in16[0]
        w_gate16_t = w_in16[1] if split_gate else w_in16_t[o_gate:]
        h, h16, h_ss = matmul(g, w2_16, _ep_half_residual, f32, extras=[(h, "tile")],
                              nk=nk_down, emit_norm=True, name="ffn_down")

        ffn2_gain = ffn2_norm[l].reshape(d, 1)
        uv = _project(h16, h_ss, w_in16_t, 0, c_uv, _ep_gelu, bf16, WIDE_TN, tm=tm,
                      name="proj_uv")
        q = _project(h16, h_ss, w_in16_t, o_q, c_q, _ep_identity, f32, WIDE_TN, tm=tm,
                     name="proj_q")
        qi = _project(h16, h_ss, w_in16_t, o_qi, c_qi, _ep_identity, f32, WIDE_TN, tm=tm,
                      name="proj_qi")
        w_keyside = jnp.concatenate(
            [w_in16_t[o_kv:o_kv + c_kv],
             jnp.pad(w_in16_t[o_small:o_gate], ((0, LANES - (o_gate - o_small)), (0, 0)))], axis=0)
        ki_gain = jnp.tile(idx_k_norm[l].reshape(1, IDX_DIM), (1, LANES // IDX_DIM))
        keyside = matmul(h16, w_keyside, functools.partial(_ep_keyside, idx_scale=idx_scale), f32,
                         tm=tm, tn=c_kv + LANES, row_sumsq=h_ss, w_rows=True,
                         extras=[(k_norm[l].reshape(1, HEAD_DIM), "const"), (cos_a, "pos"),
                                 (sin_a, "pos"), (ki_gain, "const"), (cos_i, "pos"),
                                 (sin_i, "pos")], name="proj_keyside")
        gates, w1b_16, w3b_16 = _project(
            h16, h_ss, w_gate16_t, 0, 2 * d, _ep_gate, bf16, WIDE_TN, tm=tm,
            extras=[(gate_bias[l].reshape(1, 2 * d), "row")],
            sides=[(ffn2_w1[l], ffn2_gain), (ffn2_w3[l], ffn2_gain)], name="proj_gate")

        ya = gmlp_gate(uv, gmlp_v_norm[l], gmlp_ws[l], gmlp_bs[l].T)
        yb = dsa_attention(q, qi, keyside, q_norm[l], rope_a_t, rope_i_t, batch, seq)
        m = gated_merge(ya, yb, wa16, wb16, gates, tm=tm, tn=WIDE_TN)
        h, h16, h_ss = matmul(m, wo16, _ep_residual, f32, extras=[(h, "tile")],
                              tm=tm, emit_norm=True, name="out_proj")

        g, (w2b_16,) = swiglu_up(h16, h_ss, w1b_16, w3b_16, sides=[(ffn2_w2[l], None)])
        last = l == depth - 1
        out = matmul(g, w2b_16, _ep_half_residual, f32, extras=[(h, "tile")], nk=nk_down,
                     emit_norm=not last, name="ffn_down")
        h, h16, h_ss = (out, None, None) if last else out
    return h.reshape(batch, seq, d)
```
